```python
import math
import jax, jax.numpy as jnp
from jax import lax
import numpy as np


D_MODEL = 1024
BATCH = 8
SEQ = 8192
DEPTH = 4

HEAD_DIM = 64
N_DIFF_HEADS = 4
DIFF_QK_WIDTH = N_DIFF_HEADS * 2 * HEAD_DIM
DIFF_V_WIDTH = N_DIFF_HEADS * 2 * HEAD_DIM
N_DIL_HEADS = 8
DIL_WIDTH = N_DIL_HEADS * HEAD_DIM
MIX_WIDTH = DIFF_V_WIDTH + DIL_WIDTH
PROJ_WIDTH = 2 * DIFF_QK_WIDTH + DIFF_V_WIDTH + 3 * DIL_WIDTH
DIL_PATTERNS = ((128, 1), (512, 4), (2048, 16))
Q_BLOCK = 128
D_FF = 2816
ROPE_THETA = 10000.0
EPS = 1e-6
SUBLN_EPS = 1e-5
LAMBDA_STD = 0.1
NEG_INF = -1e30

kernel_name = 'hybrid_diffattn_dilated_macaron'


def rmsnorm(x, g, eps=EPS):
    xf = x.astype(jnp.float32)
    y = xf * lax.rsqrt(jnp.mean(xf * xf, axis=-1, keepdims=True) + eps)
    return (y * g.astype(jnp.float32)).astype(x.dtype)


def swiglu(x, w_gate, w_up, w_down):
    return (jax.nn.silu(x @ w_gate) * (x @ w_up)) @ w_down


def rope_tables(positions):
    inv = 1.0 / (ROPE_THETA ** (jnp.arange(0, HEAD_DIM, 2, dtype=jnp.float32) / HEAD_DIM))
    ang = positions.astype(jnp.float32)[:, None] * inv[None, :]
    return jnp.cos(ang), jnp.sin(ang)


def apply_rope(t, cos, sin):
    t1, t2 = jnp.split(t.astype(jnp.float32), 2, axis=-1)
    c = cos[None, :, None, :]
    s = sin[None, :, None, :]
    return jnp.concatenate([t1 * c - t2 * s, t1 * s + t2 * c], axis=-1).astype(t.dtype)


def diff_attention(q, k, v, lam):
    B, S, H, _, E = q.shape
    nb = S // Q_BLOCK
    scale = E ** -0.5
    vf = v.astype(jnp.float32)
    kpos = jnp.arange(S)
    qb = q.reshape(B, nb, Q_BLOCK, H, 2, E).transpose(1, 0, 2, 3, 4, 5)

    def one_block(args):
        qblk, b = args
        s = jnp.einsum('bqhcd,bkhcd->bhcqk', qblk, k, preferred_element_type=jnp.float32) * scale
        qpos = b * Q_BLOCK + jnp.arange(Q_BLOCK)
        causal = kpos[None, :] <= qpos[:, None]
        s = jnp.where(causal[None, None, None], s, NEG_INF)
        p = jax.nn.softmax(s, axis=-1)
        a = p[:, :, 0] - lam * p[:, :, 1]
        return jnp.einsum('bhqk,bkhe->bqhe', a, vf)

    out = lax.map(one_block, (qb, jnp.arange(nb)))
    return out.transpose(1, 0, 2, 3, 4).reshape(B, S, H, 2 * E)


def dilated_branch(q, k, v, window, dilation):
    B, S, H, E = q.shape
    n = window // dilation
    L = S // dilation
    Lp = -(-L // n) * n
    nb = Lp // n

    def to_blocks(t):
        t = t.reshape(B, L, dilation, H, E).transpose(0, 2, 1, 3, 4)
        t = jnp.pad(t, ((0, 0), (0, 0), (0, Lp - L), (0, 0), (0, 0)))
        return t.reshape(B, dilation, nb, n, H, E)

    def band(t):
        prev = jnp.pad(t, ((0, 0), (0, 0), (1, 0), (0, 0), (0, 0), (0, 0)))[:, :, :-1]
        return jnp.concatenate([prev, t], axis=3)

    qb = to_blocks(q)
    kband = band(to_blocks(k))
    vband = band(to_blocks(v)).astype(jnp.float32)
    s = jnp.einsum('brnqhe,brnkhe->brnqhk', qb, kband, preferred_element_type=jnp.float32) * (E ** -0.5)
    qi = jnp.arange(n)[:, None] + n
    ki = jnp.arange(2 * n)[None, :]
    dist = qi - ki
    in_band = (dist >= 0) & (dist <= n)
    has_prev = (jnp.arange(nb) > 0)[:, None, None] | (ki >= n)[None]
    mask = in_band[None] & has_prev
    s = jnp.where(mask[None, None, :, :, None, :], s, NEG_INF)
    m = jnp.max(s, axis=-1)
    p = jnp.exp(s - m[..., None])
    den = jnp.sum(p, axis=-1)
    o = jnp.einsum('brnqhk,brnkhe->brnqhe', p, vband) / den[..., None]

    def from_blocks(t):
        t = t.reshape(B, dilation, Lp, H, -1)[:, :, :L]
        return t.transpose(0, 2, 1, 3, 4).reshape(B, S, H, -1)

    return from_blocks(o), from_blocks(m[..., None])[..., 0], from_blocks(den[..., None])[..., 0]


def dilated_mixture(q, k, v):
    outs = [dilated_branch(q, k, v, w, d) for (w, d) in DIL_PATTERNS]
    m_all = jnp.max(jnp.stack([m for (_, m, _) in outs], axis=0), axis=0)
    wts = [den * jnp.exp(m - m_all) for (_, m, den) in outs]
    num = sum(w[..., None] * o for w, (o, _, _) in zip(wts, outs))
    return num / sum(wts)[..., None]


def mixer(h, cos, sin, w_in, w_out, lq1, lk1, lq2, lk2, subln_gain, dil_gain, lam_init):
    B, S, _ = h.shape
    proj = h @ w_in
    o1 = DIFF_QK_WIDTH
    o2 = o1 + DIFF_QK_WIDTH
    o3 = o2 + DIFF_V_WIDTH
    o4 = o3 + DIL_WIDTH
    o5 = o4 + DIL_WIDTH
    dq, dk, dv, aq, ak, av = jnp.split(proj, [o1, o2, o3, o4, o5], axis=-1)

    dq = apply_rope(dq.reshape(B, S, 2 * N_DIFF_HEADS, HEAD_DIM), cos, sin).reshape(B, S, N_DIFF_HEADS, 2, HEAD_DIM)
    dk = apply_rope(dk.reshape(B, S, 2 * N_DIFF_HEADS, HEAD_DIM), cos, sin).reshape(B, S, N_DIFF_HEADS, 2, HEAD_DIM)
    dv = dv.reshape(B, S, N_DIFF_HEADS, 2 * HEAD_DIM)
    lam = (jnp.exp(jnp.sum(lq1.astype(jnp.float32) * lk1.astype(jnp.float32)))
           - jnp.exp(jnp.sum(lq2.astype(jnp.float32) * lk2.astype(jnp.float32))) + lam_init)
    d_out = diff_attention(dq, dk, dv, lam)
    d_out = rmsnorm(d_out, subln_gain, SUBLN_EPS) * (1.0 - lam_init)
    d_out = d_out.reshape(B, S, DIFF_V_WIDTH)

    aq = apply_rope(aq.reshape(B, S, N_DIL_HEADS, HEAD_DIM), cos, sin)
    ak = apply_rope(ak.reshape(B, S, N_DIL_HEADS, HEAD_DIM), cos, sin)
    av = av.reshape(B, S, N_DIL_HEADS, HEAD_DIM)
    a_out = dilated_mixture(aq, ak, av).reshape(B, S, DIL_WIDTH)
    a_out = rmsnorm(a_out, dil_gain)

    merged = jnp.concatenate([d_out, a_out], axis=-1).astype(h.dtype)
    return merged @ w_out


def setup_inputs(seed: int = 0) -> dict:
    key = jax.random.key(seed)
    ks = jax.random.split(key, 24)
    f32 = jnp.float32

    def nrm(k, shape, scale):
        return jax.random.normal(k, shape, f32) * scale

    def gain(k, shape):
        return 1.0 + 0.02 * jax.random.normal(k, shape, f32)

    return {
        'x': jax.random.normal(ks[0], (BATCH, SEQ, D_MODEL), f32),
        'positions': jnp.arange(SEQ, dtype=jnp.int32),
        'ffn1_norm': gain(ks[1], (DEPTH, D_MODEL)),
        'ffn1_gate': nrm(ks[2], (DEPTH, D_MODEL, D_FF), D_MODEL ** -0.5),
        'ffn1_up': nrm(ks[3], (DEPTH, D_MODEL, D_FF), D_MODEL ** -0.5),
        'ffn1_down': nrm(ks[4], (DEPTH, D_FF, D_MODEL), D_FF ** -0.5),
        'mix_norm': gain(ks[5], (DEPTH, D_MODEL)),
        'w_in': nrm(ks[6], (DEPTH, D_MODEL, PROJ_WIDTH), D_MODEL ** -0.5),
        'lambda_q1': nrm(ks[7], (DEPTH, HEAD_DIM), LAMBDA_STD),
        'lambda_k1': nrm(ks[8], (DEPTH, HEAD_DIM), LAMBDA_STD),
        'lambda_q2': nrm(ks[9], (DEPTH, HEAD_DIM), LAMBDA_STD),
        'lambda_k2': nrm(ks[10], (DEPTH, HEAD_DIM), LAMBDA_STD),
        'subln_gain': gain(ks[11], (DEPTH, 2 * HEAD_DIM)),
        'dil_gain': gain(ks[12], (DEPTH, DIL_WIDTH)),
        'w_out': nrm(ks[13], (DEPTH, MIX_WIDTH, D_MODEL), MIX_WIDTH ** -0.5),
        'ffn2_norm': gain(ks[14], (DEPTH, D_MODEL)),
        'ffn2_gate': nrm(ks[15], (DEPTH, D_MODEL, D_FF), D_MODEL ** -0.5),
        'ffn2_up': nrm(ks[16], (DEPTH, D_MODEL, D_FF), D_MODEL ** -0.5),
        'ffn2_down': nrm(ks[17], (DEPTH, D_FF, D_MODEL), D_FF ** -0.5),
        'final_norm': gain(ks[18], (D_MODEL,)),
    }


def reference(x, positions, ffn1_norm, ffn1_gate, ffn1_up, ffn1_down, mix_norm, w_in,
              lambda_q1, lambda_k1, lambda_q2, lambda_k2, subln_gain, dil_gain, w_out,
              ffn2_norm, ffn2_gate, ffn2_up, ffn2_down, final_norm):
    cos, sin = rope_tables(positions)
    for l in range(DEPTH):
        lam_init = 0.8 - 0.6 * math.exp(-0.3 * l)
        x = x + 0.5 * swiglu(rmsnorm(x, ffn1_norm[l]), ffn1_gate[l], ffn1_up[l], ffn1_down[l])
        x = x + mixer(rmsnorm(x, mix_norm[l]), cos, sin, w_in[l], w_out[l],
                      lambda_q1[l], lambda_k1[l], lambda_q2[l], lambda_k2[l],
                      subln_gain[l], dil_gain[l], lam_init)
        x = x + 0.5 * swiglu(rmsnorm(x, ffn2_norm[l]), ffn2_gate[l], ffn2_up[l], ffn2_down[l])
    return rmsnorm(x, final_norm)
```

```python
import functools
import math

import numpy as np
import jax
import jax.numpy as jnp
from jax import lax
from jax.experimental import pallas as pl
from jax.experimental.pallas import tpu as pltpu

D_MODEL = 1024
HEAD_DIM = 64
N_DIFF_HEADS = 4
DIFF_WIDTH = N_DIFF_HEADS * 2 * HEAD_DIM
DIL_WIDTH = 8 * HEAD_DIM
DIL_PATTERNS = ((128, 1), (512, 4), (2048, 16))
DIL_BLOCK = 128
D_FF = 2816
ROPE_THETA = 10000.0
EPS = 1e-6
SUBLN_EPS = 1e-5
NEG_INF = -1e30

LANES = 128
MXU_WIDTH = 256
VMEM_LIMIT = 56 * 1024 * 1024

QKV_WIDTH = 5 * 512
ROPE_WIDTH = 4 * 512
STATE_WIDTH = DIL_WIDTH + LANES

FF_CHUNK = 256
TOKEN_TILE = 512
DIL_ROWS = 512

BF16 = jnp.bfloat16
F32 = jnp.float32
NT_DIMS = (((1,), (1,)), ((), ()))


def _rmsnorm(xf, g, eps):
    return xf * lax.rsqrt(jnp.mean(xf * xf, axis=-1, keepdims=True) + eps) * g


def _params(semantics):
    return pltpu.CompilerParams(dimension_semantics=semantics, vmem_limit_bytes=VMEM_LIMIT)


def _resident(block_shape, index_map):
    return pl.BlockSpec(block_shape, index_map, pipeline_mode=pl.Buffered(1))


def _rope_kernel(pos_ref, inv_ref, sign_ref, cos_ref, sin_ref):
    ang = pos_ref[...].astype(F32) * inv_ref[...]
    cos_ref[...] = jnp.cos(ang)
    sin_ref[...] = jnp.sin(ang) * sign_ref[...]


def _rope_tables(positions):
    seq = positions.shape[0]
    inv = 1.0 / (ROPE_THETA ** (jnp.arange(0, HEAD_DIM, 2, dtype=F32) / HEAD_DIM))
    inv_t = jnp.tile(inv, 4)[None, :]
    sign = jnp.concatenate([-jnp.ones((64,), F32), jnp.ones((64,), F32)])[None, :]
    rows = min(seq, 1024)
    return pl.pallas_call(
        _rope_kernel,
        out_shape=(jax.ShapeDtypeStruct((seq, LANES), F32),) * 2,
        grid=(seq // rows,),
        in_specs=[pl.BlockSpec((rows, 1), lambda i: (i, 0)),
                  pl.BlockSpec((1, LANES), lambda i: (0, 0)),
                  pl.BlockSpec((1, LANES), lambda i: (0, 0))],
        out_specs=(pl.BlockSpec((rows, LANES), lambda i: (i, 0)),) * 2,
        compiler_params=_params(("arbitrary",)),
        name="rope_tables",
    )(positions.reshape(seq, 1), inv_t, sign)


def _swiglu_residual(xf, g, wg_ref, wu_ref, wd_ref, h_ref):
    xn = _rmsnorm(xf, g, EPS).astype(BF16)
    for c in range(D_FF // FF_CHUNK):
        sl = slice(c * FF_CHUNK, (c + 1) * FF_CHUNK)
        a = jnp.dot(xn, wg_ref[:, sl], preferred_element_type=F32)
        u = jnp.dot(xn, wu_ref[:, sl], preferred_element_type=F32)
        h_ref[:, sl] = (a * jax.nn.sigmoid(a) * u).astype(BF16)
    y = jnp.dot(h_ref[...], wd_ref[...], preferred_element_type=F32)
    return xf + 0.5 * y


def _ffn_kernel(*refs, has_mix, has_final):
    refs = list(refs)
    x_ref = refs.pop(0)
    if has_mix:
        d_ref, a_ref, wo_ref = refs.pop(0), refs.pop(0), refs.pop(0)
    g_ref, wg_ref, wu_ref, wd_ref = refs.pop(0), refs.pop(0), refs.pop(0), refs.pop(0)
    if has_final:
        fin_ref = refs.pop(0)
    o_ref, h_ref = refs
    xf = x_ref[...]
    if has_mix:
        xf = xf + jnp.dot(d_ref[...], wo_ref[:DIFF_WIDTH, :], preferred_element_type=F32)
        xf = xf + jnp.dot(a_ref[...], wo_ref[DIFF_WIDTH:, :], preferred_element_type=F32)
    out = _swiglu_residual(xf, g_ref[...], wg_ref, wu_ref, wd_ref, h_ref)
    if has_final:
        out = _rmsnorm(out, fin_ref[...], EPS)
    o_ref[...] = out


def _ffn_call(x2, layer, norm, wg, wu, wd, mix=None, final=None):
    n = x2.shape[0]
    tm = TOKEN_TILE
    row = lambda i: (i, 0)
    const2 = lambda i: (0, 0)
    lay3 = lambda i: (layer, 0, 0)
    args = [x2]
    specs = [pl.BlockSpec((tm, D_MODEL), row)]
    if mix is not None:
        d_out, a_out, w_out = mix
        args += [d_out, a_out, w_out]
        specs += [pl.BlockSpec((tm, DIFF_WIDTH), row), pl.BlockSpec((tm, DIL_WIDTH), row),
                  _resident((None, D_MODEL, D_MODEL), lay3)]
    args += [norm, wg, wu, wd]
    specs += [pl.BlockSpec((None, 1, D_MODEL), lay3),
              _resident((None, D_MODEL, D_FF), lay3),
              _resident((None, D_MODEL, D_FF), lay3),
              _resident((None, D_FF, D_MODEL), lay3)]
    if final is not None:
        args.append(final)
        specs.append(pl.BlockSpec((1, D_MODEL), const2))
    return pl.pallas_call(
        functools.partial(_ffn_kernel, has_mix=mix is not None, has_final=final is not None),
        out_shape=jax.ShapeDtypeStruct((n, D_MODEL), F32),
        grid=(n // tm,),
        in_specs=specs,
        out_specs=pl.BlockSpec((tm, D_MODEL), row),
        scratch_shapes=[pltpu.VMEM((tm, D_FF), BF16)],
        compiler_params=_params(("parallel",)),
        name="mix_ffn" if mix is not None else "ffn",
    )(*args)


def _proj_kernel(x_ref, g_ref, w_ref, wvt_ref, cos_ref, sin_ref, qkv_ref, vt_ref):
    xn = _rmsnorm(x_ref[...], g_ref[...], EPS).astype(BF16)
    cos = cos_ref[...]
    sin = sin_ref[...]
    for c in range(QKV_WIDTH // MXU_WIDTH):
        t = jnp.dot(xn, w_ref[:, c * MXU_WIDTH:(c + 1) * MXU_WIDTH], preferred_element_type=F32)
        if c * MXU_WIDTH < ROPE_WIDTH:
            scale = HEAD_DIM ** -0.5 if (c * MXU_WIDTH) % 1024 < 512 else 1.0
            for half in range(MXU_WIDTH // LANES):
                th = t[:, half * LANES:(half + 1) * LANES]
                r = th * cos + pltpu.roll(th, 64, 1) * sin
                lo = c * MXU_WIDTH + half * LANES
                qkv_ref[:, lo:lo + LANES] = (r * scale).astype(BF16)
        else:
            qkv_ref[:, c * MXU_WIDTH:(c + 1) * MXU_WIDTH] = t.astype(BF16)
    vt = lax.dot_general(wvt_ref[...], xn, NT_DIMS, preferred_element_type=F32)
    vt_ref[...] = vt.astype(BF16)


def _proj_call(x2, layer, norm, w_main, w_vt, cos, sin, seq):
    n = x2.shape[0]
    tm = TOKEN_TILE
    tiles_per_seq = seq // tm
    row = lambda i: (i, 0)
    lay3 = lambda i: (layer, 0, 0)
    pos = lambda i: (i % tiles_per_seq, 0)
    return pl.pallas_call(
        _proj_kernel,
        out_shape=(jax.ShapeDtypeStruct((n, QKV_WIDTH), BF16),
                   jax.ShapeDtypeStruct((n // tm, DIFF_WIDTH, tm), BF16)),
        grid=(n // tm,),
        in_specs=[pl.BlockSpec((tm, D_MODEL), row),
                  pl.BlockSpec((None, 1, D_MODEL), lay3),
                  _resident((None, D_MODEL, QKV_WIDTH), lay3),
                  _resident((None, DIFF_WIDTH, D_MODEL), lay3),
                  pl.BlockSpec((tm, LANES), pos),
                  pl.BlockSpec((tm, LANES), pos)],
        out_specs=(pl.BlockSpec((tm, QKV_WIDTH), row),
                   pl.BlockSpec((None, DIFF_WIDTH, tm), lambda i: (i, 0, 0))),
        compiler_params=_params(("parallel",)),
        name="proj_rope",
    )(x2, norm, w_main, w_vt, cos, sin)


def _diff_kernel(lam_ref, gain_ref, q_ref, k_ref, vt_ref, o_ref, acc_ref, *, lam_init, blk):
    i = pl.program_id(2)
    q = q_ref[0]
    lane = lax.broadcasted_iota(jnp.int32, (blk, LANES), 1)
    is_first = (lane % 64) < 32
    zero = jnp.zeros_like(q)
    q_comp = (jnp.where(is_first, q, zero), jnp.where(is_first, zero, q))
    acc_ref[...] = jnp.zeros_like(acc_ref)
    key_idx = lax.broadcasted_iota(jnp.int32, (blk, blk), 0)
    qry_idx = lax.broadcasted_iota(jnp.int32, (blk, blk), 1)
    causal = key_idx <= qry_idx

    def block(j, carry, masked):
        kb = k_ref[0, pl.ds(pl.multiple_of(j * blk, blk), blk), :]
        vtb = vt_ref[0, j]
        new = []
        for c in range(2):
            m_old, l_old = carry[c]
            s = lax.dot_general(kb, q_comp[c], NT_DIMS, preferred_element_type=F32)
            if masked:
                s = jnp.where(causal, s, NEG_INF)
            m_new = jnp.maximum(m_old, jnp.max(s, axis=0, keepdims=True))
            alpha = jnp.exp(m_old - m_new)
            p = jnp.exp(s - m_new)
            l_new = alpha * l_old + jnp.sum(p, axis=0, keepdims=True)
            acc_ref[c] = alpha * acc_ref[c] + jnp.dot(vtb, p.astype(BF16), preferred_element_type=F32)
            new.append((m_new, l_new))
        return tuple(new)

    start = (jnp.full((1, blk), NEG_INF, F32), jnp.zeros((1, blk), F32))
    carry = lax.fori_loop(0, i, lambda j, c: block(j, c, False), (start, start))
    (_, l0), (_, l1) = block(i, carry, True)

    lv = lam_ref[...]
    lam = (jnp.exp(jnp.sum(lv[0:1] * lv[1:2], axis=1, keepdims=True))
           - jnp.exp(jnp.sum(lv[2:3] * lv[3:4], axis=1, keepdims=True)) + lam_init)
    o = acc_ref[0] / l0 - lam * (acc_ref[1] / l1)
    y = o * lax.rsqrt(jnp.mean(o * o, axis=0, keepdims=True) + SUBLN_EPS)
    y = y * gain_ref[...] * (1.0 - lam_init)
    o_ref[0] = y.T.astype(BF16)


def _diff_call(qkv3, vt4, lam_vecs, gain_col, lam_init):
    bsz, seq, _ = qkv3.shape
    blk = TOKEN_TILE
    nblk = seq // blk
    k_lane_block = DIFF_WIDTH // LANES
    return pl.pallas_call(
        functools.partial(_diff_kernel, lam_init=lam_init, blk=blk),
        out_shape=jax.ShapeDtypeStruct((bsz, seq, DIFF_WIDTH), BF16),
        grid=(bsz, N_DIFF_HEADS, nblk),
        in_specs=[pl.BlockSpec((4, HEAD_DIM), lambda b, h, i: (0, 0)),
                  pl.BlockSpec((LANES, 1), lambda b, h, i: (0, 0)),
                  pl.BlockSpec((1, blk, LANES), lambda b, h, i: (b, i, h)),
                  pl.BlockSpec((1, seq, LANES), lambda b, h, i: (b, 0, k_lane_block + h)),
                  pl.BlockSpec((1, nblk, LANES, blk), lambda b, h, i: (b, 0, h, 0))],
        out_specs=pl.BlockSpec((1, blk, LANES), lambda b, h, i: (b, i, h)),
        scratch_shapes=[pltpu.VMEM((2, LANES, blk), F32)],
        compiler_params=_params(("parallel", "parallel", "arbitrary")),
        name="diff_attn",
    )(lam_vecs, gain_col, qkv3, qkv3, vt4)


def _dil_kernel(*refs, first, last, rows):
    refs = list(refs)
    q_ref, ko_ref, kp_ref, vo_ref, vp_ref = (refs.pop(0) for _ in range(5))
    st_ref = None if first else refs.pop(0)
    gain_ref = refs.pop(0) if last else None
    out_ref, kb_ref, vb_ref = refs
    i = pl.program_id(2)
    nsub = rows // DIL_BLOCK

    kb_ref[0:DIL_BLOCK] = kp_ref[0]
    kb_ref[DIL_BLOCK:] = ko_ref[0]
    vb_ref[0:DIL_BLOCK] = vp_ref[0]
    vb_ref[DIL_BLOCK:] = vo_ref[0]

    lane = lax.broadcasted_iota(jnp.int32, (DIL_BLOCK, LANES), 1)
    qk_first = (lane % 64) < 32
    v_first = lane < 64
    qi = lax.broadcasted_iota(jnp.int32, (DIL_BLOCK, 2 * DIL_BLOCK), 0)
    ki = lax.broadcasted_iota(jnp.int32, (DIL_BLOCK, 2 * DIL_BLOCK), 1)
    dist = qi + DIL_BLOCK - ki
    in_band = (dist >= 0) & (dist <= DIL_BLOCK)

    def sub_block(sb, _):
        r0 = pl.multiple_of(sb * DIL_BLOCK, DIL_BLOCK)
        first_key = jnp.where(i * nsub + sb > 0, 0, DIL_BLOCK)
        mask = in_band & (ki >= first_key)
        q_rows = q_ref[0, pl.ds(r0, DIL_BLOCK), :]
        k_band = kb_ref[pl.ds(r0, 2 * DIL_BLOCK), :]
        v_band = vb_ref[pl.ds(r0, 2 * DIL_BLOCK), :]
        if not first:
            st = st_ref[0, pl.ds(r0, DIL_BLOCK), :]
            st_stats = st[:, DIL_WIDTH:]
        stats = jnp.zeros((DIL_BLOCK, LANES), F32)
        pairs = []
        for g in range(DIL_WIDTH // LANES):
            sl = slice(g * LANES, (g + 1) * LANES)
            q2, k2, v2 = q_rows[:, sl], k_band[:, sl], v_band[:, sl]
            zero = jnp.zeros_like(q2)
            acc_e, den_e = [], []
            for e in range(2):
                h = 2 * g + e
                qm = jnp.where(qk_first, q2, zero) if e == 0 else jnp.where(qk_first, zero, q2)
                s = lax.dot_general(qm, k2, NT_DIMS, preferred_element_type=F32)
                s = jnp.where(mask, s, NEG_INF)
                m = jnp.max(s, axis=1, keepdims=True)
                p = jnp.exp(s - m)
                den = jnp.sum(p, axis=1, keepdims=True)
                num = jnp.dot(p.astype(BF16), v2, preferred_element_type=F32)
                if not first:
                    m_st = jnp.sum(jnp.where(lane == h, st_stats, 0.0), axis=1, keepdims=True)
                    den_st = jnp.sum(jnp.where(lane == 8 + h, st_stats, 0.0), axis=1, keepdims=True)
                    m_new = jnp.maximum(m_st, m)
                    w_st = jnp.exp(m_st - m_new)
                    w_blk = jnp.exp(m - m_new)
                    den = w_st * den_st + w_blk * den
                    num = w_st * st[:, sl] + w_blk * num
                    m = m_new
                stats = jnp.where(lane == h, m, stats)
                stats = jnp.where(lane == 8 + h, den, stats)
                acc_e.append(num)
                den_e.append(den)
            if last:
                pairs.append(jnp.where(v_first, acc_e[0] / den_e[0], acc_e[1] / den_e[1]))
            else:
                pairs.append(jnp.where(v_first, acc_e[0], acc_e[1]))
        if last:
            ssq = sum(jnp.sum(t * t, axis=1, keepdims=True) for t in pairs)
            inv = lax.rsqrt(ssq / DIL_WIDTH + EPS)
            for g, t in enumerate(pairs):
                sl = slice(g * LANES, (g + 1) * LANES)
                out_ref[0, pl.ds(r0, DIL_BLOCK), sl] = (t * inv * gain_ref[:, sl]).astype(BF16)
        else:
            for g, t in enumerate(pairs):
                out_ref[0, pl.ds(r0, DIL_BLOCK), g * LANES:(g + 1) * LANES] = t
            out_ref[0, pl.ds(r0, DIL_BLOCK), DIL_WIDTH:] = stats
        return 0

    lax.fori_loop(0, nsub, sub_block, 0)


def _dil_call(qkv3, state, gain_row, dilation, first, last):
    bsz, seq, _ = qkv3.shape
    sub_len = seq // dilation
    rows = min(DIL_ROWS, sub_len)
    nsub = rows // DIL_BLOCK
    qkv_v = qkv3.reshape(bsz, sub_len, dilation * QKV_WIDTH)
    sections = QKV_WIDTH // DIL_WIDTH
    own = lambda sec: (lambda b, r, i: (b, i, r * sections + sec))
    prev = lambda sec: (lambda b, r, i: (b, jnp.maximum(i * nsub - 1, 0), r * sections + sec))
    args = [qkv_v] * 5
    specs = [pl.BlockSpec((1, rows, DIL_WIDTH), own(2)),
             pl.BlockSpec((1, rows, DIL_WIDTH), own(3)),
             pl.BlockSpec((1, DIL_BLOCK, DIL_WIDTH), prev(3)),
             pl.BlockSpec((1, rows, DIL_WIDTH), own(4)),
             pl.BlockSpec((1, DIL_BLOCK, DIL_WIDTH), prev(4))]
    if not first:
        args.append(state.reshape(bsz, sub_len, dilation * STATE_WIDTH))
        specs.append(pl.BlockSpec((1, rows, STATE_WIDTH), lambda b, r, i: (b, i, r)))
    if last:
        args.append(gain_row)
        specs.append(pl.BlockSpec((1, DIL_WIDTH), lambda b, r, i: (0, 0)))
        out_w, out_dt = DIL_WIDTH, BF16
    else:
        out_w, out_dt = STATE_WIDTH, F32
    out = pl.pallas_call(
        functools.partial(_dil_kernel, first=first, last=last, rows=rows),
        out_shape=jax.ShapeDtypeStruct((bsz, sub_len, dilation * out_w), out_dt),
        grid=(bsz, dilation, sub_len // rows),
        in_specs=specs,
        out_specs=pl.BlockSpec((1, rows, out_w), lambda b, r, i: (b, i, r)),
        scratch_shapes=[pltpu.VMEM((rows + DIL_BLOCK, DIL_WIDTH), BF16)] * 2,
        compiler_params=_params(("parallel", "parallel", "arbitrary")),
        name=f"dilated_d{dilation}",
    )(*args)
    return out.reshape(bsz, seq, out_w)


def _rope_column_order():
    half = HEAD_DIM // 2
    cols = []
    for start in (0, 512, 1536, 2048):
        for g in range(512 // LANES):
            a = start + g * LANES
            b = a + HEAD_DIM
            cols += list(range(a, a + half)) + list(range(b, b + half))
            cols += list(range(a + half, a + HEAD_DIM)) + list(range(b + half, b + HEAD_DIM))
    cols += list(range(2560, 3072))
    return np.asarray(cols, dtype=np.int32)


def kernel(x, positions, ffn1_norm, ffn1_gate, ffn1_up, ffn1_down, mix_norm, w_in, lambda_q1, lambda_k1, lambda_q2, lambda_k2, subln_gain, dil_gain, w_out, ffn2_norm, ffn2_gate, ffn2_up, ffn2_down, final_norm):
    bsz, seq, _ = x.shape
    depth = w_in.shape[0]
    n = bsz * seq
    assert seq % 2048 == 0 and n % TOKEN_TILE == 0

    bf = lambda w: w.astype(BF16)
    w_main = bf(w_in[:, :, _rope_column_order()])
    w_vt = bf(jnp.swapaxes(w_in[:, :, 1024:1536], 1, 2))
    g1, u1, d1 = bf(ffn1_gate), bf(ffn1_up), bf(ffn1_down)
    g2, u2, d2 = bf(ffn2_gate), bf(ffn2_up), bf(ffn2_down)
    wo = bf(w_out)
    n1 = ffn1_norm[:, None, :]
    nm = mix_norm[:, None, :]
    n2 = ffn2_norm[:, None, :]

    cos, sin = _rope_tables(positions)
    h = x.reshape(n, D_MODEL)
    for l in range(depth):
        lam_init = 0.8 - 0.6 * math.exp(-0.3 * l)
        h = _ffn_call(h, l, n1, g1, u1, d1)
        qkv, vt = _proj_call(h, l, nm, w_main, w_vt, cos, sin, seq)
        qkv3 = qkv.reshape(bsz, seq, QKV_WIDTH)
        vt4 = vt.reshape(bsz, seq // TOKEN_TILE, DIFF_WIDTH, TOKEN_TILE)
        lam_vecs = jnp.stack([lambda_q1[l], lambda_k1[l], lambda_q2[l], lambda_k2[l]])
        d_out = _diff_call(qkv3, vt4, lam_vecs, subln_gain[l][:, None], lam_init)
        state = None
        for idx, (_, dilation) in enumerate(DIL_PATTERNS):
            state = _dil_call(qkv3, state, dil_gain[l][None, :], dilation,
                              first=idx == 0, last=idx == len(DIL_PATTERNS) - 1)
        final = final_norm[None, :] if l == depth - 1 else None
        h = _ffn_call(h, l, n2, g2, u2, d2,
                      mix=(d_out.reshape(n, DIFF_WIDTH), state.reshape(n, DIL_WIDTH), wo),
                      final=final)
    return h.reshape(bsz, seq, D_MODEL)
```

```python
import functools
import math

import numpy as np
import jax
import jax.numpy as jnp
from jax import lax
from jax.experimental import pallas as pl
from jax.experimental.pallas import tpu as pltpu

D_MODEL = 1024
HEAD_DIM = 64
N_DIFF_HEADS = 4
DIFF_WIDTH = N_DIFF_HEADS * 2 * HEAD_DIM
DIL_WIDTH = 8 * HEAD_DIM
DIL_PATTERNS = ((128, 1), (512, 4), (2048, 16))
DIL_BLOCK = 128
D_FF = 2816
ROPE_THETA = 10000.0
EPS = 1e-6
SUBLN_EPS = 1e-5
NEG_INF = -1e30
Q_SCALE = HEAD_DIM ** -0.5 * math.log2(math.e)

LANES = 128
MXU_WIDTH = 256
VMEM_LIMIT = 56 * 1024 * 1024

QKV_WIDTH = 5 * 512
ROPE_WIDTH = 4 * 512
STATE_WIDTH = DIL_WIDTH + LANES

FF_CHUNK = 256
TOKEN_TILE = 512
DIL_ROWS = 512

BF16 = jnp.bfloat16
F32 = jnp.float32
NT_DIMS = (((1,), (1,)), ((), ()))


def _rmsnorm(xf, g, eps):
    return xf * lax.rsqrt(jnp.mean(xf * xf, axis=-1, keepdims=True) + eps) * g


def _params(semantics):
    return pltpu.CompilerParams(dimension_semantics=semantics, vmem_limit_bytes=VMEM_LIMIT)


def _resident(block_shape, index_map):
    return pl.BlockSpec(block_shape, index_map, pipeline_mode=pl.Buffered(1))


def _rope_kernel(pos_ref, inv_ref, sign_ref, cos_ref, sin_ref):
    ang = pos_ref[...].astype(F32) * inv_ref[...]
    cos_ref[...] = jnp.cos(ang)
    sin_ref[...] = jnp.sin(ang) * sign_ref[...]


def _rope_tables(positions):
    seq = positions.shape[0]
    inv = 1.0 / (ROPE_THETA ** (jnp.arange(0, HEAD_DIM, 2, dtype=F32) / HEAD_DIM))
    inv_t = jnp.tile(inv, 4)[None, :]
    sign = jnp.concatenate([-jnp.ones((64,), F32), jnp.ones((64,), F32)])[None, :]
    rows = min(seq, 1024)
    return pl.pallas_call(
        _rope_kernel,
        out_shape=(jax.ShapeDtypeStruct((seq, LANES), F32),) * 2,
        grid=(seq // rows,),
        in_specs=[pl.BlockSpec((rows, 1), lambda i: (i, 0)),
                  pl.BlockSpec((1, LANES), lambda i: (0, 0)),
                  pl.BlockSpec((1, LANES), lambda i: (0, 0))],
        out_specs=(pl.BlockSpec((rows, LANES), lambda i: (i, 0)),) * 2,
        compiler_params=_params(("arbitrary",)),
        name="rope_tables",
    )(positions.reshape(seq, 1), inv_t, sign)


def _swiglu_residual(xf, g, wg_ref, wu_ref, wd_ref, h_ref):
    xn = _rmsnorm(xf, g, EPS).astype(BF16)
    for c in range(D_FF // FF_CHUNK):
        sl = slice(c * FF_CHUNK, (c + 1) * FF_CHUNK)
        a = jnp.dot(xn, wg_ref[:, sl], preferred_element_type=F32)
        u = jnp.dot(xn, wu_ref[:, sl], preferred_element_type=F32)
        h_ref[:, sl] = (a * jax.nn.sigmoid(a) * u).astype(BF16)
    y = jnp.dot(h_ref[...], wd_ref[...], preferred_element_type=F32)
    return xf + 0.5 * y


def _ffn_kernel(*refs, has_mix, has_final):
    refs = list(refs)
    x_ref = refs.pop(0)
    if has_mix:
        d_ref, a_ref, wo_ref = refs.pop(0), refs.pop(0), refs.pop(0)
    g_ref, wg_ref, wu_ref, wd_ref = refs.pop(0), refs.pop(0), refs.pop(0), refs.pop(0)
    if has_final:
        fin_ref = refs.pop(0)
    o_ref, h_ref = refs
    xf = x_ref[...]
    if has_mix:
        xf = xf + jnp.dot(d_ref[...], wo_ref[:DIFF_WIDTH, :], preferred_element_type=F32)
        xf = xf + jnp.dot(a_ref[...], wo_ref[DIFF_WIDTH:, :], preferred_element_type=F32)
    out = _swiglu_residual(xf, g_ref[...], wg_ref, wu_ref, wd_ref, h_ref)
    if has_final:
        out = _rmsnorm(out, fin_ref[...], EPS)
    o_ref[...] = out


def _ffn_call(x2, layer, norm, wg, wu, wd, mix=None, final=None):
    n = x2.shape[0]
    tm = TOKEN_TILE
    row = lambda i: (i, 0)
    const2 = lambda i: (0, 0)
    lay3 = lambda i: (layer, 0, 0)
    args = [x2]
    specs = [pl.BlockSpec((tm, D_MODEL), row)]
    if mix is not None:
        d_out, a_out, w_out = mix
        args += [d_out, a_out, w_out]
        specs += [pl.BlockSpec((tm, DIFF_WIDTH), row), pl.BlockSpec((tm, DIL_WIDTH), row),
                  _resident((None, D_MODEL, D_MODEL), lay3)]
    args += [norm, wg, wu, wd]
    specs += [pl.BlockSpec((None, 1, D_MODEL), lay3),
              _resident((None, D_MODEL, D_FF), lay3),
              _resident((None, D_MODEL, D_FF), lay3),
              _resident((None, D_FF, D_MODEL), lay3)]
    if final is not None:
        args.append(final)
        specs.append(pl.BlockSpec((1, D_MODEL), const2))
    return pl.pallas_call(
        functools.partial(_ffn_kernel, has_mix=mix is not None, has_final=final is not None),
        out_shape=jax.ShapeDtypeStruct((n, D_MODEL), F32),
        grid=(n // tm,),
        in_specs=specs,
        out_specs=pl.BlockSpec((tm, D_MODEL), row),
        scratch_shapes=[pltpu.VMEM((tm, D_FF), BF16)],
        compiler_params=_params(("parallel",)),
        name="mix_ffn" if mix is not None else "ffn",
    )(*args)


def _proj_kernel(x_ref, g_ref, w_ref, wvt_ref, cos_ref, sin_ref, qkv_ref, vt_ref):
    xn = _rmsnorm(x_ref[...], g_ref[...], EPS).astype(BF16)
    cos = cos_ref[...]
    sin = sin_ref[...]
    for c in range(QKV_WIDTH // MXU_WIDTH):
        t = jnp.dot(xn, w_ref[:, c * MXU_WIDTH:(c + 1) * MXU_WIDTH], preferred_element_type=F32)
        if c * MXU_WIDTH < ROPE_WIDTH:
            scale = Q_SCALE if (c * MXU_WIDTH) % 1024 < 512 else 1.0
            for half in range(MXU_WIDTH // LANES):
                th = t[:, half * LANES:(half + 1) * LANES]
                r = th * cos + pltpu.roll(th, 64, 1) * sin
                lo = c * MXU_WIDTH + half * LANES
                qkv_ref[:, lo:lo + LANES] = (r * scale).astype(BF16)
        else:
            qkv_ref[:, c * MXU_WIDTH:(c + 1) * MXU_WIDTH] = t.astype(BF16)
    vt = lax.dot_general(wvt_ref[...], xn, NT_DIMS, preferred_element_type=F32)
    vt_ref[...] = vt.astype(BF16)


def _proj_call(x2, layer, norm, w_main, w_vt, cos, sin, seq):
    n = x2.shape[0]
    tm = TOKEN_TILE
    tiles_per_seq = seq // tm
    row = lambda i: (i, 0)
    lay3 = lambda i: (layer, 0, 0)
    pos = lambda i: (i % tiles_per_seq, 0)
    return pl.pallas_call(
        _proj_kernel,
        out_shape=(jax.ShapeDtypeStruct((n, QKV_WIDTH), BF16),
                   jax.ShapeDtypeStruct((n // tm, DIFF_WIDTH, tm), BF16)),
        grid=(n // tm,),
        in_specs=[pl.BlockSpec((tm, D_MODEL), row),
                  pl.BlockSpec((None, 1, D_MODEL), lay3),
                  _resident((None, D_MODEL, QKV_WIDTH), lay3),
                  _resident((None, DIFF_WIDTH, D_MODEL), lay3),
                  pl.BlockSpec((tm, LANES), pos),
                  pl.BlockSpec((tm, LANES), pos)],
        out_specs=(pl.BlockSpec((tm, QKV_WIDTH), row),
                   pl.BlockSpec((None, DIFF_WIDTH, tm), lambda i: (i, 0, 0))),
        compiler_params=_params(("parallel",)),
        name="proj_rope",
    )(x2, norm, w_main, w_vt, cos, sin)


def _diff_kernel(lam_ref, gain_ref, q_ref, k_ref, vt_ref, o_ref, qall_ref, acc_ref, *, lam_init, blk):
    i = pl.program_id(2)
    ncol = 4 * blk
    lane = lax.broadcasted_iota(jnp.int32, (blk, LANES), 1)
    is_first = (lane % 64) < 32
    for hh in range(2):
        q = q_ref[0, hh * blk:(hh + 1) * blk, :]
        zero = jnp.zeros_like(q)
        qall_ref[(2 * hh) * blk:(2 * hh + 1) * blk, :] = jnp.where(is_first, q, zero)
        qall_ref[(2 * hh + 1) * blk:(2 * hh + 2) * blk, :] = jnp.where(is_first, zero, q)
    acc_ref[...] = jnp.zeros_like(acc_ref)

    def block(j, nkb, carry, col0, masked_cols):
        m_old, l_old = carry
        keys = nkb * blk
        kb = k_ref[0, pl.ds(pl.multiple_of(j * blk, blk), keys), :]
        width = ncol - col0
        s = lax.dot_general(kb, qall_ref[col0:, :], NT_DIMS, preferred_element_type=F32)
        if masked_cols:
            key_idx = lax.broadcasted_iota(jnp.int32, (keys, width), 0)
            col_idx = lax.broadcasted_iota(jnp.int32, (keys, width), 1)
            visible = (key_idx <= col_idx % blk) | (col_idx >= masked_cols)
            s = jnp.where(visible, s, NEG_INF)
        m_prev = m_old[:, col0:]
        m_new = jnp.maximum(m_prev, jnp.max(s, axis=0, keepdims=True))
        alpha = jnp.exp2(m_prev - m_new)
        p = jnp.exp2(s - m_new)
        l_new = alpha * l_old[:, col0:] + jnp.sum(p, axis=0, keepdims=True)
        pb = p.astype(BF16)
        pv = sum(jnp.dot(vt_ref[0, j + t], pb[t * blk:(t + 1) * blk], preferred_element_type=F32)
                 for t in range(nkb))
        acc_ref[:, col0:] = alpha * acc_ref[:, col0:] + pv
        if col0:
            m_new = jnp.concatenate([m_old[:, :col0], m_new], axis=1)
            l_new = jnp.concatenate([l_old[:, :col0], l_new], axis=1)
        return m_new, l_new

    start = (jnp.full((1, ncol), NEG_INF, F32), jnp.zeros((1, ncol), F32))
    carry = lax.fori_loop(0, i, lambda jj, c: block(2 * jj, 2, c, 0, 0), start)
    carry = block(2 * i, 1, carry, 0, 2 * blk)
    _, l = block(2 * i + 1, 1, carry, 2 * blk, 2 * blk)

    lv = lam_ref[...]
    lam = (jnp.exp(jnp.sum(lv[0:1] * lv[1:2], axis=1, keepdims=True))
           - jnp.exp(jnp.sum(lv[2:3] * lv[3:4], axis=1, keepdims=True)) + lam_init)
    for hh in range(2):
        c0 = slice((2 * hh) * blk, (2 * hh + 1) * blk)
        c1 = slice((2 * hh + 1) * blk, (2 * hh + 2) * blk)
        o = acc_ref[:, c0] / l[:, c0] - lam * (acc_ref[:, c1] / l[:, c1])
        y = o * lax.rsqrt(jnp.mean(o * o, axis=0, keepdims=True) + SUBLN_EPS)
        y = y * gain_ref[...] * (1.0 - lam_init)
        o_ref[0, hh * blk:(hh + 1) * blk, :] = y.T.astype(BF16)


def _diff_call(qkv3, vt4, lam_vecs, gain_col, lam_init):
    bsz, seq, _ = qkv3.shape
    blk = TOKEN_TILE
    nblk = seq // blk
    k_lane_block = DIFF_WIDTH // LANES
    return pl.pallas_call(
        functools.partial(_diff_kernel, lam_init=lam_init, blk=blk),
        out_shape=jax.ShapeDtypeStruct((bsz, seq, DIFF_WIDTH), BF16),
        grid=(bsz, N_DIFF_HEADS, nblk // 2),
        in_specs=[pl.BlockSpec((4, HEAD_DIM), lambda b, h, i: (0, 0)),
                  pl.BlockSpec((LANES, 1), lambda b, h, i: (0, 0)),
                  pl.BlockSpec((1, 2 * blk, LANES), lambda b, h, i: (b, i, h)),
                  pl.BlockSpec((1, seq, LANES), lambda b, h, i: (b, 0, k_lane_block + h)),
                  pl.BlockSpec((1, nblk, LANES, blk), lambda b, h, i: (b, 0, h, 0))],
        out_specs=pl.BlockSpec((1, 2 * blk, LANES), lambda b, h, i: (b, i, h)),
        scratch_shapes=[pltpu.VMEM((4 * blk, LANES), BF16), pltpu.VMEM((LANES, 4 * blk), F32)],
        compiler_params=_params(("parallel", "parallel", "arbitrary")),
        name="diff_attn",
    )(lam_vecs, gain_col, qkv3, qkv3, vt4)


def _dil_kernel(*refs, first, last, rows):
    refs = list(refs)
    q_ref, ko_ref, kp_ref, vo_ref, vp_ref = (refs.pop(0) for _ in range(5))
    st_ref = None if first else refs.pop(0)
    gain_ref = refs.pop(0) if last else None
    out_ref, kb_ref, vb_ref = refs
    i = pl.program_id(2)
    nsub = rows // DIL_BLOCK

    kb_ref[0:DIL_BLOCK] = kp_ref[0]
    kb_ref[DIL_BLOCK:] = ko_ref[0]
    vb_ref[0:DIL_BLOCK] = vp_ref[0]
    vb_ref[DIL_BLOCK:] = vo_ref[0]

    lane = lax.broadcasted_iota(jnp.int32, (DIL_BLOCK, LANES), 1)
    qk_first = (lane % 64) < 32
    v_first = lane < 64
    qi = lax.broadcasted_iota(jnp.int32, (DIL_BLOCK, 2 * DIL_BLOCK), 0)
    ki = lax.broadcasted_iota(jnp.int32, (DIL_BLOCK, 2 * DIL_BLOCK), 1)
    dist = qi + DIL_BLOCK - ki
    in_band = (dist >= 0) & (dist <= DIL_BLOCK)

    def sub_block(sb, _):
        r0 = pl.multiple_of(sb * DIL_BLOCK, DIL_BLOCK)
        first_key = jnp.where(i * nsub + sb > 0, 0, DIL_BLOCK)
        mask = in_band & (ki >= first_key)
        q_rows = q_ref[0, pl.ds(r0, DIL_BLOCK), :]
        k_band = kb_ref[pl.ds(r0, 2 * DIL_BLOCK), :]
        v_band = vb_ref[pl.ds(r0, 2 * DIL_BLOCK), :]
        if not first:
            st = st_ref[0, pl.ds(r0, DIL_BLOCK), :]
            st_stats = st[:, DIL_WIDTH:]
        stats = jnp.zeros((DIL_BLOCK, LANES), F32)
        pairs = []
        for g in range(DIL_WIDTH // LANES):
            sl = slice(g * LANES, (g + 1) * LANES)
            q2, k2, v2 = q_rows[:, sl], k_band[:, sl], v_band[:, sl]
            zero = jnp.zeros_like(q2)
            acc_e, den_e = [], []
            for e in range(2):
                h = 2 * g + e
                qm = jnp.where(qk_first, q2, zero) if e == 0 else jnp.where(qk_first, zero, q2)
                s = lax.dot_general(qm, k2, NT_DIMS, preferred_element_type=F32)
                s = jnp.where(mask, s, NEG_INF)
                m = jnp.max(s, axis=1, keepdims=True)
                p = jnp.exp2(s - m)
                den = jnp.sum(p, axis=1, keepdims=True)
                num = jnp.dot(p.astype(BF16), v2, preferred_element_type=F32)
                if not first:
                    m_st = jnp.sum(jnp.where(lane == h, st_stats, 0.0), axis=1, keepdims=True)
                    den_st = jnp.sum(jnp.where(lane == 8 + h, st_stats, 0.0), axis=1, keepdims=True)
                    m_new = jnp.maximum(m_st, m)
                    w_st = jnp.exp2(m_st - m_new)
                    w_blk = jnp.exp2(m - m_new)
                    den = w_st * den_st + w_blk * den
                    num = w_st * st[:, sl] + w_blk * num
                    m = m_new
                stats = jnp.where(lane == h, m, stats)
                stats = jnp.where(lane == 8 + h, den, stats)
                acc_e.append(num)
                den_e.append(den)
            if last:
                pairs.append(jnp.where(v_first, acc_e[0] / den_e[0], acc_e[1] / den_e[1]))
            else:
                pairs.append(jnp.where(v_first, acc_e[0], acc_e[1]))
        if last:
            ssq = sum(jnp.sum(t * t, axis=1, keepdims=True) for t in pairs)
            inv = lax.rsqrt(ssq / DIL_WIDTH + EPS)
            for g, t in enumerate(pairs):
                sl = slice(g * LANES, (g + 1) * LANES)
                out_ref[0, pl.ds(r0, DIL_BLOCK), sl] = (t * inv * gain_ref[:, sl]).astype(BF16)
        else:
            for g, t in enumerate(pairs):
                out_ref[0, pl.ds(r0, DIL_BLOCK), g * LANES:(g + 1) * LANES] = t
            out_ref[0, pl.ds(r0, DIL_BLOCK), DIL_WIDTH:] = stats
        return 0

    lax.fori_loop(0, nsub, sub_block, 0)


def _dil_call(qkv3, state, gain_row, dilation, first, last):
    bsz, seq, _ = qkv3.shape
    sub_len = seq // dilation
    rows = min(DIL_ROWS, sub_len)
    nsub = rows // DIL_BLOCK
    qkv_v = qkv3.reshape(bsz, sub_len, dilation * QKV_WIDTH)
    sections = QKV_WIDTH // DIL_WIDTH
    own = lambda sec: (lambda b, r, i: (b, i, r * sections + sec))
    prev = lambda sec: (lambda b, r, i: (b, jnp.maximum(i * nsub - 1, 0), r * sections + sec))
    args = [qkv_v] * 5
    specs = [pl.BlockSpec((1, rows, DIL_WIDTH), own(2)),
             pl.BlockSpec((1, rows, DIL_WIDTH), own(3)),
             pl.BlockSpec((1, DIL_BLOCK, DIL_WIDTH), prev(3)),
             pl.BlockSpec((1, rows, DIL_WIDTH), own(4)),
             pl.BlockSpec((1, DIL_BLOCK, DIL_WIDTH), prev(4))]
    if not first:
        args.append(state.reshape(bsz, sub_len, dilation * STATE_WIDTH))
        specs.append(pl.BlockSpec((1, rows, STATE_WIDTH), lambda b, r, i: (b, i, r)))
    if last:
        args.append(gain_row)
        specs.append(pl.BlockSpec((1, DIL_WIDTH), lambda b, r, i: (0, 0)))
        out_w, out_dt = DIL_WIDTH, BF16
    else:
        out_w, out_dt = STATE_WIDTH, F32
    out = pl.pallas_call(
        functools.partial(_dil_kernel, first=first, last=last, rows=rows),
        out_shape=jax.ShapeDtypeStruct((bsz, sub_len, dilation * out_w), out_dt),
        grid=(bsz, dilation, sub_len // rows),
        in_specs=specs,
        out_specs=pl.BlockSpec((1, rows, out_w), lambda b, r, i: (b, i, r)),
        scratch_shapes=[pltpu.VMEM((rows + DIL_BLOCK, DIL_WIDTH), BF16)] * 2,
        compiler_params=_params(("parallel", "parallel", "arbitrary")),
        name=f"dilated_d{dilation}",
    )(*args)
    return out.reshape(bsz, seq, out_w)


def _rope_column_order():
    half = HEAD_DIM // 2
    cols = []
    for start in (0, 512, 1536, 2048):
        for g in range(512 // LANES):
            a = start + g * LANES
            b = a + HEAD_DIM
            cols += list(range(a, a + half)) + list(range(b, b + half))
            cols += list(range(a + half, a + HEAD_DIM)) + list(range(b + half, b + HEAD_DIM))
    cols += list(range(2560, 3072))
    return np.asarray(cols, dtype=np.int32)


def kernel(x, positions, ffn1_norm, ffn1_gate, ffn1_up, ffn1_down, mix_norm, w_in, lambda_q1, lambda_k1, lambda_q2, lambda_k2, subln_gain, dil_gain, w_out, ffn2_norm, ffn2_gate, ffn2_up, ffn2_down, final_norm):
    bsz, seq, _ = x.shape
    depth = w_in.shape[0]
    n = bsz * seq
    assert seq % 2048 == 0 and n % TOKEN_TILE == 0

    bf = lambda w: w.astype(BF16)
    w_main = bf(w_in[:, :, _rope_column_order()])
    w_vt = bf(jnp.swapaxes(w_in[:, :, 1024:1536], 1, 2))
    g1, u1, d1 = bf(ffn1_gate), bf(ffn1_up), bf(ffn1_down)
    g2, u2, d2 = bf(ffn2_gate), bf(ffn2_up), bf(ffn2_down)
    wo = bf(w_out)
    n1 = ffn1_norm[:, None, :]
    nm = mix_norm[:, None, :]
    n2 = ffn2_norm[:, None, :]

    cos, sin = _rope_tables(positions)
    h = x.reshape(n, D_MODEL)
    for l in range(depth):
        lam_init = 0.8 - 0.6 * math.exp(-0.3 * l)
        h = _ffn_call(h, l, n1, g1, u1, d1)
        qkv, vt = _proj_call(h, l, nm, w_main, w_vt, cos, sin, seq)
        qkv3 = qkv.reshape(bsz, seq, QKV_WIDTH)
        vt4 = vt.reshape(bsz, seq // TOKEN_TILE, DIFF_WIDTH, TOKEN_TILE)
        lam_vecs = jnp.stack([lambda_q1[l], lambda_k1[l], lambda_q2[l], lambda_k2[l]])
        d_out = _diff_call(qkv3, vt4, lam_vecs, subln_gain[l][:, None], lam_init)
        state = None
        for idx, (_, dilation) in enumerate(DIL_PATTERNS):
            state = _dil_call(qkv3, state, dil_gain[l][None, :], dilation,
                              first=idx == 0, last=idx == len(DIL_PATTERNS) - 1)
        final = final_norm[None, :] if l == depth - 1 else None
        h = _ffn_call(h, l, n2, g2, u2, d2,
                      mix=(d_out.reshape(n, DIFF_WIDTH), state.reshape(n, DIL_WIDTH), wo),
                      final=final)
    return h.reshape(bsz, seq, D_MODEL)
```

```python
import functools
import math

import numpy as np
import jax
import jax.numpy as jnp
from jax import lax
from jax.experimental import pallas as pl
from jax.experimental.pallas import tpu as pltpu

D_MODEL = 1024
HEAD_DIM = 64
N_DIFF_HEADS = 4
DIFF_WIDTH = N_DIFF_HEADS * 2 * HEAD_DIM
DIL_WIDTH = 8 * HEAD_DIM
DIL_BLOCK = 128
DIL_MAX = 16
D_FF = 2816
ROPE_THETA = 10000.0
EPS = 1e-6
SUBLN_EPS = 1e-5
NEG_INF = -1e30
Q_SCALE = HEAD_DIM ** -0.5 * math.log2(math.e)

LANES = 128
SUBLANES = 8
MXU_WIDTH = 256
VMEM_LIMIT = 56 * 1024 * 1024

QK_WIDTH = 2 * DIFF_WIDTH
DIL_QKV_WIDTH = 3 * DIL_WIDTH
PROJ_WIDTH = QK_WIDTH + DIL_QKV_WIDTH
ROPE_WIDTH = 4 * 512

FF_CHUNK = 256
TOKEN_TILE = 512
DIL_CHUNK = DIL_BLOCK * DIL_MAX
DIL_GROUP = 8

BF16 = jnp.bfloat16
F32 = jnp.float32
NT_DIMS = (((1,), (1,)), ((), ()))


def _rmsnorm(xf, g, eps):
    return xf * lax.rsqrt(jnp.mean(xf * xf, axis=-1, keepdims=True) + eps) * g


def _params(semantics):
    return pltpu.CompilerParams(dimension_semantics=semantics, vmem_limit_bytes=VMEM_LIMIT)


def _resident(block_shape, index_map):
    return pl.BlockSpec(block_shape, index_map, pipeline_mode=pl.Buffered(1))


def _true_pos(idx):
    return (idx // DIL_BLOCK) * DIL_BLOCK + DIL_MAX * (idx % SUBLANES) + (idx % DIL_BLOCK) // SUBLANES


def _to_residue_major(a, axis):
    shape = a.shape
    a = a.reshape(shape[:axis] + (shape[axis] // DIL_BLOCK, SUBLANES, DIL_MAX) + shape[axis + 1:])
    return jnp.swapaxes(a, axis + 1, axis + 2).reshape(shape)


def _from_residue_major(a, axis):
    shape = a.shape
    a = a.reshape(shape[:axis] + (shape[axis] // DIL_BLOCK, DIL_MAX, SUBLANES) + shape[axis + 1:])
    return jnp.swapaxes(a, axis + 1, axis + 2).reshape(shape)


def _rope_kernel(pos_ref, inv_ref, sign_ref, cos_ref, sin_ref):
    ang = pos_ref[...].astype(F32) * inv_ref[...]
    cos_ref[...] = jnp.cos(ang)
    sin_ref[...] = jnp.sin(ang) * sign_ref[...]


def _rope_tables(positions):
    seq = positions.shape[0]
    inv = 1.0 / (ROPE_THETA ** (jnp.arange(0, HEAD_DIM, 2, dtype=F32) / HEAD_DIM))
    inv_t = jnp.tile(inv, 4)[None, :]
    sign = jnp.concatenate([-jnp.ones((64,), F32), jnp.ones((64,), F32)])[None, :]
    rows = min(seq, 1024)
    return pl.pallas_call(
        _rope_kernel,
        out_shape=(jax.ShapeDtypeStruct((seq, LANES), F32),) * 2,
        grid=(seq // rows,),
        in_specs=[pl.BlockSpec((rows, 1), lambda i: (i, 0)),
                  pl.BlockSpec((1, LANES), lambda i: (0, 0)),
                  pl.BlockSpec((1, LANES), lambda i: (0, 0))],
        out_specs=(pl.BlockSpec((rows, LANES), lambda i: (i, 0)),) * 2,
        compiler_params=_params(("arbitrary",)),
        name="rope_tables",
    )(positions.reshape(seq, 1), inv_t, sign)


def _swiglu_residual(xf, g, wg_ref, wu_ref, wd_ref, h_ref):
    xn = _rmsnorm(xf, g, EPS).astype(BF16)
    for c in range(D_FF // FF_CHUNK):
        sl = slice(c * FF_CHUNK, (c + 1) * FF_CHUNK)
        a = jnp.dot(xn, wg_ref[:, sl], preferred_element_type=F32)
        u = jnp.dot(xn, wu_ref[:, sl], preferred_element_type=F32)
        h_ref[:, sl] = (a * jax.nn.sigmoid(a) * u).astype(BF16)
    y = jnp.dot(h_ref[...], wd_ref[...], preferred_element_type=F32)
    return xf + 0.5 * y


def _ffn_kernel(*refs, has_mix, has_final):
    refs = list(refs)
    x_ref = refs.pop(0)
    if has_mix:
        d_ref, a_ref, dg_ref, wo_ref = (refs.pop(0) for _ in range(4))
    g_ref, wg_ref, wu_ref, wd_ref = (refs.pop(0) for _ in range(4))
    if has_final:
        fin_ref = refs.pop(0)
    o_ref, h_ref = refs
    xf = x_ref[...]
    if has_mix:
        a_out = _rmsnorm(a_ref[...], dg_ref[...], EPS).astype(BF16)
        xf = xf + jnp.dot(d_ref[...], wo_ref[:DIFF_WIDTH, :], preferred_element_type=F32)
        xf = xf + jnp.dot(a_out, wo_ref[DIFF_WIDTH:, :], preferred_element_type=F32)
    out = _swiglu_residual(xf, g_ref[...], wg_ref, wu_ref, wd_ref, h_ref)
    if has_final:
        out = _rmsnorm(out, fin_ref[...], EPS)
    o_ref[...] = out


def _ffn_call(x2, layer, norm, wg, wu, wd, mix=None, final=None):
    n = x2.shape[0]
    tm = TOKEN_TILE
    row = lambda i: (i, 0)
    const2 = lambda i: (0, 0)
    lay3 = lambda i: (layer, 0, 0)
    args = [x2]
    specs = [pl.BlockSpec((tm, D_MODEL), row)]
    if mix is not None:
        d_out, a_mix, dil_gain, w_out = mix
        args += [d_out, a_mix, dil_gain, w_out]
        specs += [pl.BlockSpec((tm, DIFF_WIDTH), row), pl.BlockSpec((tm, DIL_WIDTH), row),
                  pl.BlockSpec((None, 1, DIL_WIDTH), lay3),
                  _resident((None, D_MODEL, D_MODEL), lay3)]
    args += [norm, wg, wu, wd]
    specs += [pl.BlockSpec((None, 1, D_MODEL), lay3),
              _resident((None, D_MODEL, D_FF), lay3),
              _resident((None, D_MODEL, D_FF), lay3),
              _resident((None, D_FF, D_MODEL), lay3)]
    if final is not None:
        args.append(final)
        specs.append(pl.BlockSpec((1, D_MODEL), const2))
    return pl.pallas_call(
        functools.partial(_ffn_kernel, has_mix=mix is not None, has_final=final is not None),
        out_shape=jax.ShapeDtypeStruct((n, D_MODEL), F32),
        grid=(n // tm,),
        in_specs=specs,
        out_specs=pl.BlockSpec((tm, D_MODEL), row),
        scratch_shapes=[pltpu.VMEM((tm, D_FF), BF16)],
        compiler_params=_params(("parallel",)),
        name="mix_ffn" if mix is not None else "ffn",
    )(*args)


def _proj_kernel(x_ref, g_ref, w_ref, wvt_ref, cos_ref, sin_ref, qk_ref, dil_ref, vt_ref):
    xn = _rmsnorm(x_ref[...], g_ref[...], EPS).astype(BF16)
    cos = cos_ref[...]
    sin = sin_ref[...]
    for c in range(PROJ_WIDTH // MXU_WIDTH):
        lo = c * MXU_WIDTH
        t = jnp.dot(xn, w_ref[:, lo:lo + MXU_WIDTH], preferred_element_type=F32)
        if lo < ROPE_WIDTH:
            scale = Q_SCALE if lo % 1024 < 512 else 1.0
            halves = []
            for half in range(MXU_WIDTH // LANES):
                th = t[:, half * LANES:(half + 1) * LANES]
                halves.append((th * cos + pltpu.roll(th, 64, 1) * sin) * scale)
            t = jnp.concatenate(halves, axis=1)
        if lo < QK_WIDTH:
            qk_ref[:, lo:lo + MXU_WIDTH] = t.astype(BF16)
        else:
            dil_ref[:, lo - QK_WIDTH:lo - QK_WIDTH + MXU_WIDTH] = t
    vt = lax.dot_general(wvt_ref[...], xn, NT_DIMS, preferred_element_type=F32)
    vt_ref[...] = vt.astype(BF16)


def _proj_call(x2, layer, norm, w_main, w_vt, cos, sin, seq):
    n = x2.shape[0]
    tm = TOKEN_TILE
    tiles_per_seq = seq // tm
    row = lambda i: (i, 0)
    lay3 = lambda i: (layer, 0, 0)
    pos = lambda i: (i % tiles_per_seq, 0)
    return pl.pallas_call(
        _proj_kernel,
        out_shape=(jax.ShapeDtypeStruct((n, QK_WIDTH), BF16),
                   jax.ShapeDtypeStruct((n, DIL_QKV_WIDTH), F32),
                   jax.ShapeDtypeStruct((n // tm, DIFF_WIDTH, tm), BF16)),
        grid=(n // tm,),
        in_specs=[pl.BlockSpec((tm, D_MODEL), row),
                  pl.BlockSpec((None, 1, D_MODEL), lay3),
                  _resident((None, D_MODEL, PROJ_WIDTH), lay3),
                  _resident((None, DIFF_WIDTH, D_MODEL), lay3),
                  pl.BlockSpec((tm, LANES), pos),
                  pl.BlockSpec((tm, LANES), pos)],
        out_specs=(pl.BlockSpec((tm, QK_WIDTH), row),
                   pl.BlockSpec((tm, DIL_QKV_WIDTH), row),
                   pl.BlockSpec((None, DIFF_WIDTH, tm), lambda i: (i, 0, 0))),
        compiler_params=_params(("parallel",)),
        name="proj_rope",
    )(x2, norm, w_main, w_vt, cos, sin)


def _diff_kernel(lam_ref, gain_ref, q_ref, k_ref, vt_ref, o_ref, qall_ref, acc_ref, *, lam_init, blk):
    i = pl.program_id(2)
    ncol = 4 * blk
    lane = lax.broadcasted_iota(jnp.int32, (blk, LANES), 1)
    is_first = (lane % 64) < 32
    for hh in range(2):
        q = q_ref[0, hh * blk:(hh + 1) * blk, :]
        zero = jnp.zeros_like(q)
        qall_ref[(2 * hh) * blk:(2 * hh + 1) * blk, :] = jnp.where(is_first, q, zero)
        qall_ref[(2 * hh + 1) * blk:(2 * hh + 2) * blk, :] = jnp.where(is_first, zero, q)
    acc_ref[...] = jnp.zeros_like(acc_ref)

    def block(j, nkb, carry, col0, masked_cols):
        m_old, l_old = carry
        keys = nkb * blk
        kb = k_ref[0, pl.ds(pl.multiple_of(j * blk, blk), keys), :]
        width = ncol - col0
        s = lax.dot_general(kb, qall_ref[col0:, :], NT_DIMS, preferred_element_type=F32)
        if masked_cols:
            key_idx = lax.broadcasted_iota(jnp.int32, (keys, width), 0)
            col_idx = lax.broadcasted_iota(jnp.int32, (keys, width), 1)
            visible = (_true_pos(key_idx) <= _true_pos(col_idx % blk)) | (col_idx >= masked_cols)
            s = jnp.where(visible, s, NEG_INF)
        m_prev = m_old[:, col0:]
        m_new = jnp.maximum(m_prev, jnp.max(s, axis=0, keepdims=True))
        alpha = jnp.exp2(m_prev - m_new)
        p = jnp.exp2(s - m_new)
        l_new = alpha * l_old[:, col0:] + jnp.sum(p, axis=0, keepdims=True)
        pb = p.astype(BF16)
        pv = sum(jnp.dot(vt_ref[0, j + t], pb[t * blk:(t + 1) * blk], preferred_element_type=F32)
                 for t in range(nkb))
        acc_ref[:, col0:] = alpha * acc_ref[:, col0:] + pv
        if col0:
            m_new = jnp.concatenate([m_old[:, :col0], m_new], axis=1)
            l_new = jnp.concatenate([l_old[:, :col0], l_new], axis=1)
        return m_new, l_new

    start = (jnp.full((1, ncol), NEG_INF, F32), jnp.zeros((1, ncol), F32))
    carry = lax.fori_loop(0, i, lambda jj, c: block(2 * jj, 2, c, 0, 0), start)
    carry = block(2 * i, 1, carry, 0, 2 * blk)
    _, l = block(2 * i + 1, 1, carry, 2 * blk, 2 * blk)

    lv = lam_ref[...]
    lam = (jnp.exp(jnp.sum(lv[0:1] * lv[1:2], axis=1, keepdims=True))
           - jnp.exp(jnp.sum(lv[2:3] * lv[3:4], axis=1, keepdims=True)) + lam_init)
    for hh in range(2):
        c0 = slice((2 * hh) * blk, (2 * hh + 1) * blk)
        c1 = slice((2 * hh + 1) * blk, (2 * hh + 2) * blk)
        o = acc_ref[:, c0] / l[:, c0] - lam * (acc_ref[:, c1] / l[:, c1])
        y = o * lax.rsqrt(jnp.mean(o * o, axis=0, keepdims=True) + SUBLN_EPS)
        y = y * gain_ref[...] * (1.0 - lam_init)
        o_ref[0, hh * blk:(hh + 1) * blk, :] = y.T.astype(BF16)


def _diff_call(qk3, vt4, lam_vecs, gain_col, lam_init):
    bsz, seq, _ = qk3.shape
    blk = TOKEN_TILE
    nblk = seq // blk
    k_lane_block = DIFF_WIDTH // LANES
    return pl.pallas_call(
        functools.partial(_diff_kernel, lam_init=lam_init, blk=blk),
        out_shape=jax.ShapeDtypeStruct((bsz, seq, DIFF_WIDTH), BF16),
        grid=(bsz, N_DIFF_HEADS, nblk // 2),
        in_specs=[pl.BlockSpec((4, HEAD_DIM), lambda b, h, i: (0, 0)),
                  pl.BlockSpec((LANES, 1), lambda b, h, i: (0, 0)),
                  pl.BlockSpec((1, 2 * blk, LANES), lambda b, h, i: (b, i, h)),
                  pl.BlockSpec((1, seq, LANES), lambda b, h, i: (b, 0, k_lane_block + h)),
                  pl.BlockSpec((1, nblk, LANES, blk), lambda b, h, i: (b, 0, h, 0))],
        out_specs=pl.BlockSpec((1, 2 * blk, LANES), lambda b, h, i: (b, i, h)),
        scratch_shapes=[pltpu.VMEM((4 * blk, LANES), BF16), pltpu.VMEM((LANES, 4 * blk), F32)],
        compiler_params=_params(("parallel", "parallel", "arbitrary")),
        name="diff_attn",
    )(lam_vecs, gain_col, qk3, qk3, vt4)


def _dil_kernel(q_ref, k_ref, v_ref, o_ref, m_ref, l_ref, acc_ref):
    c = pl.program_id(2)
    base = c * DIL_CHUNK
    blk = DIL_BLOCK
    lane = lax.broadcasted_iota(jnp.int32, (blk, LANES), 1)
    v_first = lane < 64
    lane2 = lax.broadcasted_iota(jnp.int32, (2 * blk, LANES), 1)
    row2 = lax.broadcasted_iota(jnp.int32, (2 * blk, LANES), 0)
    q_keep = ((lane2 % 64) < 32) == (row2 < blk)
    qrow = lax.broadcasted_iota(jnp.int32, (2 * blk, 2 * blk), 0) % blk
    kcol = lax.broadcasted_iota(jnp.int32, (2 * blk, 2 * blk), 1)

    def band(qpos, kpos):
        dist = qpos + blk - kpos
        return (dist >= 0) & (dist <= blk)

    pos4 = lambda idx: 32 * (idx // 32) + 4 * (idx % SUBLANES) + (idx % 32) // SUBLANES
    band1 = band(_true_pos(qrow), _true_pos(kcol))
    band4 = band(pos4(qrow), blk * (kcol // blk) + pos4(kcol % blk))
    band16 = band(qrow, kcol)

    def strips(ref, starts):
        return jnp.concatenate([ref[0, pl.ds(pl.multiple_of(s, SUBLANES), SUBLANES), :] for s in starts], axis=0)

    def attend(q, k, v, mask):
        qb = q.astype(BF16)
        q2 = jnp.concatenate([qb, qb], axis=0)
        q2 = jnp.where(q_keep, q2, jnp.zeros_like(q2))
        s = lax.dot_general(q2, k.astype(BF16), NT_DIMS, preferred_element_type=F32)
        s = jnp.where(mask, s, NEG_INF)
        m = jnp.max(s, axis=1, keepdims=True)
        p = jnp.exp2(s - m)
        den = jnp.sum(p, axis=1, keepdims=True)
        num = jnp.dot(p.astype(BF16), v.astype(BF16), preferred_element_type=F32)
        pair = lambda t: jnp.where(v_first, jnp.broadcast_to(t[:blk], (blk, LANES)),
                                   jnp.broadcast_to(t[blk:], (blk, LANES)))
        return pair(m), pair(den), jnp.where(v_first, num[:blk], num[blk:])

    def merge(q_starts, m, den, num):
        for t, s0 in enumerate(q_starts):
            rows = pl.ds(pl.multiple_of(s0, SUBLANES), SUBLANES)
            sl = slice(t * SUBLANES, (t + 1) * SUBLANES)
            m_st = m_ref[rows, :]
            m_new = jnp.maximum(m_st, m[sl])
            w_st = jnp.exp2(m_st - m_new)
            w_new = jnp.exp2(m[sl] - m_new)
            m_ref[rows, :] = m_new
            l_ref[rows, :] = w_st * l_ref[rows, :] + w_new * den[sl]
            acc_ref[rows, :] = w_st * acc_ref[rows, :] + w_new * num[sl]

    def d1_group(gi, _):
        for t in range(DIL_GROUP):
            u = gi * DIL_GROUP + t
            own = base + u * blk
            prev = jnp.maximum(own - blk, 0)
            first_key = jnp.where(own > 0, 0, blk)
            rows = pl.ds(pl.multiple_of(u * blk, blk), blk)
            kv = lambda ref: jnp.concatenate([ref[0, pl.ds(pl.multiple_of(prev, blk), blk), :],
                                              ref[0, pl.ds(pl.multiple_of(own, blk), blk), :]], axis=0)
            m, den, num = attend(q_ref[0, rows, :], kv(k_ref), kv(v_ref), band1 & (kcol >= first_key))
            m_ref[rows, :] = m
            l_ref[rows, :] = den
            acc_ref[rows, :] = num
        return 0

    lax.fori_loop(0, DIL_MAX // DIL_GROUP, d1_group, 0)

    def d4_group(gi, _):
        for r4, bq in [(gi * (DIL_GROUP // 4) + t, bq) for t in range(DIL_GROUP // 4) for bq in range(4)]:
            rel = [blk * (4 * bq + ul) + SUBLANES * (4 * a + r4) for ul in range(4) for a in range(4)]
            own = [base + s for s in rel]
            pstart = jnp.maximum(base + 512 * (bq - 1), 0) - 512 * bq
            prev = [pstart + s for s in rel]
            first_key = jnp.where(base + 512 * bq > 0, 0, blk)
            k = jnp.concatenate([strips(k_ref, prev), strips(k_ref, own)], axis=0)
            v = jnp.concatenate([strips(v_ref, prev), strips(v_ref, own)], axis=0)
            m, den, num = attend(strips(q_ref, rel), k, v, band4 & (kcol >= first_key))
            merge(rel, m, den, num)
        return 0

    lax.fori_loop(0, DIL_MAX // DIL_GROUP, d4_group, 0)

    def d16_group(gi, _):
        for t in range(DIL_GROUP):
            r = gi * DIL_GROUP + t
            rel = [blk * u + SUBLANES * r for u in range(DIL_MAX)]
            own = [base + s for s in rel]
            pbase = jnp.maximum(base - DIL_CHUNK, 0)
            prev = [pbase + s for s in rel]
            first_key = jnp.where(base > 0, 0, blk)
            k = jnp.concatenate([strips(k_ref, prev), strips(k_ref, own)], axis=0)
            v = jnp.concatenate([strips(v_ref, prev), strips(v_ref, own)], axis=0)
            m, den, num = attend(strips(q_ref, rel), k, v, band16 & (kcol >= first_key))
            merge(rel, m, den, num)
        return 0

    lax.fori_loop(0, DIL_MAX // DIL_GROUP, d16_group, 0)

    o_ref[0] = acc_ref[...] / l_ref[...]


def _dil_call(dil3):
    bsz, seq, _ = dil3.shape
    pairs = DIL_WIDTH // LANES
    return pl.pallas_call(
        _dil_kernel,
        out_shape=jax.ShapeDtypeStruct((bsz, seq, DIL_WIDTH), F32),
        grid=(bsz, pairs, seq // DIL_CHUNK),
        in_specs=[pl.BlockSpec((1, DIL_CHUNK, LANES), lambda b, g, c: (b, c, g)),
                  pl.BlockSpec((1, seq, LANES), lambda b, g, c: (b, 0, pairs + g)),
                  pl.BlockSpec((1, seq, LANES), lambda b, g, c: (b, 0, 2 * pairs + g))],
        out_specs=pl.BlockSpec((1, DIL_CHUNK, LANES), lambda b, g, c: (b, c, g)),
        scratch_shapes=[pltpu.VMEM((DIL_CHUNK, LANES), F32)] * 3,
        compiler_params=_params(("parallel", "parallel", "arbitrary")),
        name="dilated",
    )(dil3, dil3, dil3)


def _rope_column_order():
    half = HEAD_DIM // 2
    cols = []
    for start in (0, 512, 1536, 2048):
        for g in range(512 // LANES):
            a = start + g * LANES
            b = a + HEAD_DIM
            cols += list(range(a, a + half)) + list(range(b, b + half))
            cols += list(range(a + half, a + HEAD_DIM)) + list(range(b + half, b + HEAD_DIM))
    cols += list(range(2560, 3072))
    return np.asarray(cols, dtype=np.int32)


def kernel(x, positions, ffn1_norm, ffn1_gate, ffn1_up, ffn1_down, mix_norm, w_in, lambda_q1, lambda_k1, lambda_q2, lambda_k2, subln_gain, dil_gain, w_out, ffn2_norm, ffn2_gate, ffn2_up, ffn2_down, final_norm):
    bsz, seq, _ = x.shape
    depth = w_in.shape[0]
    n = bsz * seq
    assert seq % DIL_CHUNK == 0 and seq % (2 * TOKEN_TILE) == 0

    bf = lambda w: w.astype(BF16)
    w_main = bf(w_in[:, :, _rope_column_order()])
    w_vt = bf(jnp.swapaxes(w_in[:, :, 1024:1536], 1, 2))
    g1, u1, d1 = bf(ffn1_gate), bf(ffn1_up), bf(ffn1_down)
    g2, u2, d2 = bf(ffn2_gate), bf(ffn2_up), bf(ffn2_down)
    wo = bf(w_out)
    n1 = ffn1_norm[:, None, :]
    nm = mix_norm[:, None, :]
    n2 = ffn2_norm[:, None, :]
    dg = dil_gain[:, None, :]

    cos, sin = _rope_tables(_to_residue_major(positions, 0))
    h = _to_residue_major(x, 1).reshape(n, D_MODEL)
    for l in range(depth):
        lam_init = 0.8 - 0.6 * math.exp(-0.3 * l)
        h = _ffn_call(h, l, n1, g1, u1, d1)
        qk, dil, vt = _proj_call(h, l, nm, w_main, w_vt, cos, sin, seq)
        vt4 = vt.reshape(bsz, seq // TOKEN_TILE, DIFF_WIDTH, TOKEN_TILE)
        lam_vecs = jnp.stack([lambda_q1[l], lambda_k1[l], lambda_q2[l], lambda_k2[l]])
        d_out = _diff_call(qk.reshape(bsz, seq, QK_WIDTH), vt4, lam_vecs, subln_gain[l][:, None], lam_init)
        a_mix = _dil_call(dil.reshape(bsz, seq, DIL_QKV_WIDTH))
        final = final_norm[None, :] if l == depth - 1 else None
        h = _ffn_call(h, l, n2, g2, u2, d2,
                      mix=(d_out.reshape(n, DIFF_WIDTH), a_mix.reshape(n, DIL_WIDTH), dg, wo),
                      final=final)
    return _from_residue_major(h.reshape(bsz, seq, D_MODEL), 1)
```

```python
import functools
import math

import numpy as np
import jax
import jax.numpy as jnp
from jax import lax
from jax.experimental import pallas as pl
from jax.experimental.pallas import tpu as pltpu

D_MODEL = 1024
HEAD_DIM = 64
N_DIFF_HEADS = 4
DIFF_WIDTH = N_DIFF_HEADS * 2 * HEAD_DIM
DIL_WIDTH = 8 * HEAD_DIM
DIL_BLOCK = 128
DIL_MAX = 16
D_FF = 2816
ROPE_THETA = 10000.0
EPS = 1e-6
SUBLN_EPS = 1e-5
NEG_INF = -1e30
Q_SCALE = HEAD_DIM ** -0.5 * math.log2(math.e)
LAZY_LIMIT = 2.0 ** 90

LANES = 128
SUBLANES = 8
MXU_WIDTH = 256
VMEM_LIMIT = 56 * 1024 * 1024

QK_WIDTH = 2 * DIFF_WIDTH
DIL_QKV_WIDTH = 3 * DIL_WIDTH
PROJ_WIDTH = QK_WIDTH + DIL_QKV_WIDTH
ROPE_WIDTH = 4 * 512

FF_CHUNK = 256
TOKEN_TILE = 512
VT_ROWS = LANES + 16
DIL_CHUNK = DIL_BLOCK * DIL_MAX

BF16 = jnp.bfloat16
F32 = jnp.float32
NT_DIMS = (((1,), (1,)), ((), ()))


def _rmsnorm(xf, g, eps):
    return xf * lax.rsqrt(jnp.mean(xf * xf, axis=-1, keepdims=True) + eps) * g


def _params(semantics):
    return pltpu.CompilerParams(dimension_semantics=semantics, vmem_limit_bytes=VMEM_LIMIT)


def _resident(block_shape, index_map):
    return pl.BlockSpec(block_shape, index_map, pipeline_mode=pl.Buffered(1))


def _true_pos(idx):
    return (idx // DIL_BLOCK) * DIL_BLOCK + DIL_MAX * (idx % SUBLANES) + (idx % DIL_BLOCK) // SUBLANES


def _to_residue_major(a, axis):
    shape = a.shape
    a = a.reshape(shape[:axis] + (shape[axis] // DIL_BLOCK, SUBLANES, DIL_MAX) + shape[axis + 1:])
    return jnp.swapaxes(a, axis + 1, axis + 2).reshape(shape)


def _from_residue_major(a, axis):
    shape = a.shape
    a = a.reshape(shape[:axis] + (shape[axis] // DIL_BLOCK, DIL_MAX, SUBLANES) + shape[axis + 1:])
    return jnp.swapaxes(a, axis + 1, axis + 2).reshape(shape)


def _rope_kernel(pos_ref, inv_ref, sign_ref, cos_ref, sin_ref):
    ang = pos_ref[...].astype(F32) * inv_ref[...]
    cos_ref[...] = jnp.cos(ang)
    sin_ref[...] = jnp.sin(ang) * sign_ref[...]


def _rope_tables(positions):
    seq = positions.shape[0]
    inv = 1.0 / (ROPE_THETA ** (jnp.arange(0, HEAD_DIM, 2, dtype=F32) / HEAD_DIM))
    inv_t = jnp.tile(inv, 4)[None, :]
    sign = jnp.concatenate([-jnp.ones((64,), F32), jnp.ones((64,), F32)])[None, :]
    rows = min(seq, 1024)
    return pl.pallas_call(
        _rope_kernel,
        out_shape=(jax.ShapeDtypeStruct((seq, LANES), F32),) * 2,
        grid=(seq // rows,),
        in_specs=[pl.BlockSpec((rows, 1), lambda i: (i, 0)),
                  pl.BlockSpec((1, LANES), lambda i: (0, 0)),
                  pl.BlockSpec((1, LANES), lambda i: (0, 0))],
        out_specs=(pl.BlockSpec((rows, LANES), lambda i: (i, 0)),) * 2,
        compiler_params=_params(("arbitrary",)),
        name="rope_tables",
    )(positions.reshape(seq, 1), inv_t, sign)


def _swiglu_residual(xf, g, wg_ref, wu_ref, wd_ref, h_ref):
    xn = _rmsnorm(xf, g, EPS).astype(BF16)
    for c in range(D_FF // FF_CHUNK):
        sl = slice(c * FF_CHUNK, (c + 1) * FF_CHUNK)
        a = jnp.dot(xn, wg_ref[:, sl], preferred_element_type=F32)
        u = jnp.dot(xn, wu_ref[:, sl], preferred_element_type=F32)
        h_ref[:, sl] = (a * jax.nn.sigmoid(a) * u).astype(BF16)
    y = jnp.dot(h_ref[...], wd_ref[...], preferred_element_type=F32)
    return xf + 0.5 * y


def _ffn_kernel(*refs, has_mix, has_final):
    refs = list(refs)
    x_ref = refs.pop(0)
    if has_mix:
        d_ref, a_ref, dg_ref, wo_ref = (refs.pop(0) for _ in range(4))
    g_ref, wg_ref, wu_ref, wd_ref = (refs.pop(0) for _ in range(4))
    if has_final:
        fin_ref = refs.pop(0)
    o_ref, h_ref = refs
    xf = x_ref[...]
    if has_mix:
        a_out = _rmsnorm(a_ref[...], dg_ref[...], EPS).astype(BF16)
        xf = xf + jnp.dot(d_ref[...], wo_ref[:DIFF_WIDTH, :], preferred_element_type=F32)
        xf = xf + jnp.dot(a_out, wo_ref[DIFF_WIDTH:, :], preferred_element_type=F32)
    out = _swiglu_residual(xf, g_ref[...], wg_ref, wu_ref, wd_ref, h_ref)
    if has_final:
        out = _rmsnorm(out, fin_ref[...], EPS)
    o_ref[...] = out


def _ffn_call(x2, layer, norm, wg, wu, wd, mix=None, final=None):
    n = x2.shape[0]
    tm = TOKEN_TILE
    row = lambda i: (i, 0)
    const2 = lambda i: (0, 0)
    lay3 = lambda i: (layer, 0, 0)
    args = [x2]
    specs = [pl.BlockSpec((tm, D_MODEL), row)]
    if mix is not None:
        d_out, a_mix, dil_gain, w_out = mix
        args += [d_out, a_mix, dil_gain, w_out]
        specs += [pl.BlockSpec((tm, DIFF_WIDTH), row), pl.BlockSpec((tm, DIL_WIDTH), row),
                  pl.BlockSpec((None, 1, DIL_WIDTH), lay3),
                  _resident((None, D_MODEL, D_MODEL), lay3)]
    args += [norm, wg, wu, wd]
    specs += [pl.BlockSpec((None, 1, D_MODEL), lay3),
              _resident((None, D_MODEL, D_FF), lay3),
              _resident((None, D_MODEL, D_FF), lay3),
              _resident((None, D_FF, D_MODEL), lay3)]
    if final is not None:
        args.append(final)
        specs.append(pl.BlockSpec((1, D_MODEL), const2))
    return pl.pallas_call(
        functools.partial(_ffn_kernel, has_mix=mix is not None, has_final=final is not None),
        out_shape=jax.ShapeDtypeStruct((n, D_MODEL), F32),
        grid=(n // tm,),
        in_specs=specs,
        out_specs=pl.BlockSpec((tm, D_MODEL), row),
        scratch_shapes=[pltpu.VMEM((tm, D_FF), BF16)],
        compiler_params=_params(("parallel",)),
        name="mix_ffn" if mix is not None else "ffn",
    )(*args)


def _proj_kernel(x_ref, g_ref, w_ref, wvt_ref, cos_ref, sin_ref, qk_ref, dil_ref, vt_ref):
    xn = _rmsnorm(x_ref[...], g_ref[...], EPS).astype(BF16)
    cos = cos_ref[...]
    sin = sin_ref[...]
    for c in range(PROJ_WIDTH // MXU_WIDTH):
        lo = c * MXU_WIDTH
        t = jnp.dot(xn, w_ref[:, lo:lo + MXU_WIDTH], preferred_element_type=F32)
        if lo < ROPE_WIDTH:
            scale = Q_SCALE if lo % 1024 < 512 else 1.0
            halves = []
            for half in range(MXU_WIDTH // LANES):
                th = t[:, half * LANES:(half + 1) * LANES]
                halves.append((th * cos + pltpu.roll(th, 64, 1) * sin) * scale)
            t = jnp.concatenate(halves, axis=1)
        if lo < QK_WIDTH:
            qk_ref[:, lo:lo + MXU_WIDTH] = t.astype(BF16)
        else:
            dil_ref[:, lo - QK_WIDTH:lo - QK_WIDTH + MXU_WIDTH] = t
    vt = lax.dot_general(wvt_ref[...], xn, NT_DIMS, preferred_element_type=F32)
    for h in range(N_DIFF_HEADS):
        vt_ref[h * VT_ROWS:h * VT_ROWS + LANES, :] = vt[h * LANES:(h + 1) * LANES].astype(BF16)
        vt_ref[h * VT_ROWS + LANES:(h + 1) * VT_ROWS, :] = jnp.ones((VT_ROWS - LANES, vt.shape[1]), BF16)


def _proj_call(x2, layer, norm, w_main, w_vt, cos, sin, seq):
    n = x2.shape[0]
    tm = TOKEN_TILE
    tiles_per_seq = seq // tm
    row = lambda i: (i, 0)
    lay3 = lambda i: (layer, 0, 0)
    pos = lambda i: (i % tiles_per_seq, 0)
    return pl.pallas_call(
        _proj_kernel,
        out_shape=(jax.ShapeDtypeStruct((n, QK_WIDTH), BF16),
                   jax.ShapeDtypeStruct((n, DIL_QKV_WIDTH), F32),
                   jax.ShapeDtypeStruct((n // tm, N_DIFF_HEADS * VT_ROWS, tm), BF16)),
        grid=(n // tm,),
        in_specs=[pl.BlockSpec((tm, D_MODEL), row),
                  pl.BlockSpec((None, 1, D_MODEL), lay3),
                  _resident((None, D_MODEL, PROJ_WIDTH), lay3),
                  _resident((None, DIFF_WIDTH, D_MODEL), lay3),
                  pl.BlockSpec((tm, LANES), pos),
                  pl.BlockSpec((tm, LANES), pos)],
        out_specs=(pl.BlockSpec((tm, QK_WIDTH), row),
                   pl.BlockSpec((tm, DIL_QKV_WIDTH), row),
                   pl.BlockSpec((None, N_DIFF_HEADS * VT_ROWS, tm), lambda i: (i, 0, 0))),
        compiler_params=_params(("parallel",)),
        name="proj_rope",
    )(x2, norm, w_main, w_vt, cos, sin)


def _diff_kernel(lam_ref, gain_ref, q_ref, k_ref, vt_ref, o_ref, qall_ref, acc_ref, *, lam_init, blk):
    i = pl.program_id(2)
    ncol = 4 * blk
    lane = lax.broadcasted_iota(jnp.int32, (blk, LANES), 1)
    is_first = (lane % 64) < 32
    for hh in range(2):
        q = q_ref[0, hh * blk:(hh + 1) * blk, :]
        zero = jnp.zeros_like(q)
        qall_ref[(2 * hh) * blk:(2 * hh + 1) * blk, :] = jnp.where(is_first, q, zero)
        qall_ref[(2 * hh + 1) * blk:(2 * hh + 2) * blk, :] = jnp.where(is_first, zero, q)
    acc_ref[...] = jnp.zeros_like(acc_ref)

    def block(j, nkb, m_old, col0, masked_cols):
        keys = nkb * blk
        kb = k_ref[0, pl.ds(pl.multiple_of(j * blk, blk), keys), :]
        width = ncol - col0
        s = lax.dot_general(kb, qall_ref[col0:, :], NT_DIMS, preferred_element_type=F32)
        if masked_cols:
            key_idx = lax.broadcasted_iota(jnp.int32, (keys, width), 0)
            col_idx = lax.broadcasted_iota(jnp.int32, (keys, width), 1)
            visible = (_true_pos(key_idx) <= _true_pos(col_idx % blk)) | (col_idx >= masked_cols)
            s = jnp.where(visible, s, NEG_INF)
        m_prev = m_old[:, col0:]
        m_new = jnp.maximum(m_prev, jnp.max(s, axis=0, keepdims=True))
        alpha = jnp.exp2(m_prev - m_new)
        p = jnp.exp2(s - m_new).astype(BF16)
        pv = sum(jnp.dot(vt_ref[0, j + t], p[t * blk:(t + 1) * blk], preferred_element_type=F32)
                 for t in range(nkb))
        acc_ref[:, col0:] = alpha * acc_ref[:, col0:] + pv
        if col0:
            m_new = jnp.concatenate([m_old[:, :col0], m_new], axis=1)
        return m_new

    def lazy_block(jj, m):
        j = 2 * jj
        kb = k_ref[0, pl.ds(pl.multiple_of(j * blk, blk), 2 * blk), :]
        s = lax.dot_general(kb, qall_ref[...], NT_DIMS, preferred_element_type=F32)
        p = jnp.exp2(s - m).astype(BF16)
        pv = (jnp.dot(vt_ref[0, j], p[:blk], preferred_element_type=F32)
              + jnp.dot(vt_ref[0, j + 1], p[blk:], preferred_element_type=F32))
        too_big = jnp.max(pv[LANES:LANES + 1, :]) > LAZY_LIMIT

        def fast(_):
            acc_ref[...] += pv
            return m

        return lax.cond(too_big, lambda _: block(j, 2, m, 0, 0), fast, None)

    m = jnp.full((1, ncol), NEG_INF, F32)
    m = block(2 * i, 1, m, 0, 2 * blk)
    m = block(2 * i + 1, 1, m, 2 * blk, 2 * blk)
    lax.fori_loop(0, i, lazy_block, m)

    lv = lam_ref[...]
    lam = (jnp.exp(jnp.sum(lv[0:1] * lv[1:2], axis=1, keepdims=True))
           - jnp.exp(jnp.sum(lv[2:3] * lv[3:4], axis=1, keepdims=True)) + lam_init)
    for hh in range(2):
        c0 = slice((2 * hh) * blk, (2 * hh + 1) * blk)
        c1 = slice((2 * hh + 1) * blk, (2 * hh + 2) * blk)
        o = (acc_ref[:LANES, c0] / acc_ref[LANES:LANES + 1, c0]
             - lam * (acc_ref[:LANES, c1] / acc_ref[LANES:LANES + 1, c1]))
        y = o * lax.rsqrt(jnp.mean(o * o, axis=0, keepdims=True) + SUBLN_EPS)
        y = y * gain_ref[...] * (1.0 - lam_init)
        o_ref[0, hh * blk:(hh + 1) * blk, :] = y.T.astype(BF16)


def _diff_call(qk3, vt4, lam_vecs, gain_col, lam_init):
    bsz, seq, _ = qk3.shape
    blk = TOKEN_TILE
    nblk = seq // blk
    k_lane_block = DIFF_WIDTH // LANES
    return pl.pallas_call(
        functools.partial(_diff_kernel, lam_init=lam_init, blk=blk),
        out_shape=jax.ShapeDtypeStruct((bsz, seq, DIFF_WIDTH), BF16),
        grid=(bsz, N_DIFF_HEADS, nblk // 2),
        in_specs=[pl.BlockSpec((4, HEAD_DIM), lambda b, h, i: (0, 0)),
                  pl.BlockSpec((LANES, 1), lambda b, h, i: (0, 0)),
                  pl.BlockSpec((1, 2 * blk, LANES), lambda b, h, i: (b, i, h)),
                  pl.BlockSpec((1, seq, LANES), lambda b, h, i: (b, 0, k_lane_block + h)),
                  pl.BlockSpec((1, nblk, VT_ROWS, blk), lambda b, h, i: (b, 0, h, 0))],
        out_specs=pl.BlockSpec((1, 2 * blk, LANES), lambda b, h, i: (b, i, h)),
        scratch_shapes=[pltpu.VMEM((4 * blk, LANES), BF16), pltpu.VMEM((VT_ROWS, 4 * blk), F32)],
        compiler_params=_params(("parallel", "parallel", "arbitrary")),
        name="diff_attn",
    )(lam_vecs, gain_col, qk3, qk3, vt4)


def _dil_kernel(q_ref, k_ref, v_ref, o_ref, m_ref, l_ref, acc_ref):
    c = pl.program_id(2)
    base = c * DIL_CHUNK
    blk = DIL_BLOCK
    lane = lax.broadcasted_iota(jnp.int32, (blk, LANES), 1)
    v_first = lane < 64
    lane2 = lax.broadcasted_iota(jnp.int32, (2 * blk, LANES), 1)
    row2 = lax.broadcasted_iota(jnp.int32, (2 * blk, LANES), 0)
    q_keep = ((lane2 % 64) < 32) == (row2 < blk)
    qrow = lax.broadcasted_iota(jnp.int32, (2 * blk, 2 * blk), 0) % blk
    kcol = lax.broadcasted_iota(jnp.int32, (2 * blk, 2 * blk), 1)

    def band(qpos, kpos):
        dist = qpos + blk - kpos
        return (dist >= 0) & (dist <= blk)

    def score_cap(visible):
        return jnp.where(visible, jnp.inf, NEG_INF).astype(F32)

    pos4 = lambda idx: 32 * (idx // 32) + 4 * (idx % SUBLANES) + (idx % 32) // SUBLANES
    band1 = band(_true_pos(qrow), _true_pos(kcol))
    band4 = band(pos4(qrow), blk * (kcol // blk) + pos4(kcol % blk))
    band16 = band(qrow, kcol)

    def strip(start):
        return pl.ds(start if isinstance(start, int) else pl.multiple_of(start, SUBLANES), SUBLANES)

    def strips(ref, starts):
        return jnp.concatenate([ref[0, strip(s), :] for s in starts], axis=0)

    def attend(q, k, v, cap):
        qb = q.astype(BF16)
        q2 = jnp.concatenate([qb, qb], axis=0)
        q2 = jnp.where(q_keep, q2, jnp.zeros_like(q2))
        s = lax.dot_general(q2, k, NT_DIMS, preferred_element_type=F32)
        s = jnp.minimum(s, cap)
        m = jnp.max(s, axis=1, keepdims=True)
        p = jnp.exp2(s - m).astype(BF16)
        v_ones = jnp.concatenate([v, jnp.ones_like(v)], axis=1)
        num = jnp.dot(p, v_ones, preferred_element_type=F32)
        m_pair = jnp.where(v_first, jnp.broadcast_to(m[:blk], (blk, LANES)), jnp.broadcast_to(m[blk:], (blk, LANES)))
        return (m_pair, jnp.where(v_first, num[:blk, LANES:], num[blk:, LANES:]),
                jnp.where(v_first, num[:blk, :LANES], num[blk:, :LANES]))

    def merge(q_starts, m, den, num):
        for t, s0 in enumerate(q_starts):
            rows = strip(s0)
            sl = slice(t * SUBLANES, (t + 1) * SUBLANES)
            m_st = m_ref[rows, :]
            m_new = jnp.maximum(m_st, m[sl])
            w_st = jnp.exp2(m_st - m_new)
            w_new = jnp.exp2(m[sl] - m_new)
            m_ref[rows, :] = m_new
            l_ref[rows, :] = w_st * l_ref[rows, :] + w_new * den[sl]
            acc_ref[rows, :] = w_st * acc_ref[rows, :] + w_new * num[sl]

    first_key = jnp.where(c > 0, 0, blk)

    cap_first = lambda in_band: score_cap(in_band & (kcol >= first_key))
    both = lambda prev, own: jnp.concatenate([prev, own], axis=0)
    gather = lambda ref, starts: strips(ref, starts).astype(BF16)

    cap_a, cap_b = cap_first(band1), score_cap(band1)
    start0 = jnp.maximum(base - blk, 0)
    load = lambda ref, start: ref[0, pl.ds(pl.multiple_of(start, blk), blk), :].astype(BF16)
    k_prev, v_prev = load(k_ref, start0), load(v_ref, start0)
    for u in range(DIL_MAX):
        rows = slice(u * blk, (u + 1) * blk)
        k_own, v_own = load(k_ref, base + u * blk), load(v_ref, base + u * blk)
        m, den, num = attend(q_ref[0, rows, :], both(k_prev, k_own), both(v_prev, v_own), cap_a if u == 0 else cap_b)
        m_ref[rows, :] = m
        l_ref[rows, :] = den
        acc_ref[rows, :] = num
        k_prev, v_prev = k_own, v_own

    cap_a, cap_b = cap_first(band4), score_cap(band4)
    start0 = jnp.maximum(base - 512, 0)
    for r4 in range(4):
        block_rows = lambda bq: [blk * (4 * bq + ul) + SUBLANES * (4 * a + r4) for ul in range(4) for a in range(4)]
        prev = [start0 + s for s in block_rows(0)]
        k_prev, v_prev = gather(k_ref, prev), gather(v_ref, prev)
        for bq in range(4):
            rel = block_rows(bq)
            own = [base + s for s in rel]
            k_own, v_own = gather(k_ref, own), gather(v_ref, own)
            m, den, num = attend(strips(q_ref, rel), both(k_prev, k_own), both(v_prev, v_own),
                                 cap_a if bq == 0 else cap_b)
            merge(rel, m, den, num)
            k_prev, v_prev = k_own, v_own

    cap_a = cap_first(band16)
    start0 = jnp.maximum(base - DIL_CHUNK, 0)
    for r in range(DIL_MAX):
        rel = [blk * u + SUBLANES * r for u in range(DIL_MAX)]
        k = both(gather(k_ref, [start0 + s for s in rel]), gather(k_ref, [base + s for s in rel]))
        v = both(gather(v_ref, [start0 + s for s in rel]), gather(v_ref, [base + s for s in rel]))
        m, den, num = attend(strips(q_ref, rel), k, v, cap_a)
        merge(rel, m, den, num)

    o_ref[0] = acc_ref[...] / l_ref[...]


def _dil_call(dil3):
    bsz, seq, _ = dil3.shape
    pairs = DIL_WIDTH // LANES
    return pl.pallas_call(
        _dil_kernel,
        out_shape=jax.ShapeDtypeStruct((bsz, seq, DIL_WIDTH), F32),
        grid=(bsz, pairs, seq // DIL_CHUNK),
        in_specs=[pl.BlockSpec((1, DIL_CHUNK, LANES), lambda b, g, c: (b, c, g)),
                  pl.BlockSpec((1, seq, LANES), lambda b, g, c: (b, 0, pairs + g)),
                  pl.BlockSpec((1, seq, LANES), lambda b, g, c: (b, 0, 2 * pairs + g))],
        out_specs=pl.BlockSpec((1, DIL_CHUNK, LANES), lambda b, g, c: (b, c, g)),
        scratch_shapes=[pltpu.VMEM((DIL_CHUNK, LANES), F32)] * 3,
        compiler_params=_params(("parallel", "parallel", "arbitrary")),
        name="dilated",
    )(dil3, dil3, dil3)


def _rope_column_order():
    half = HEAD_DIM // 2
    cols = []
    for start in (0, 512, 1536, 2048):
        for g in range(512 // LANES):
            a = start + g * LANES
            b = a + HEAD_DIM
            cols += list(range(a, a + half)) + list(range(b, b + half))
            cols += list(range(a + half, a + HEAD_DIM)) + list(range(b + half, b + HEAD_DIM))
    cols += list(range(2560, 3072))
    return np.asarray(cols, dtype=np.int32)


def kernel(x, positions, ffn1_norm, ffn1_gate, ffn1_up, ffn1_down, mix_norm, w_in, lambda_q1, lambda_k1, lambda_q2, lambda_k2, subln_gain, dil_gain, w_out, ffn2_norm, ffn2_gate, ffn2_up, ffn2_down, final_norm):
    bsz, seq, _ = x.shape
    depth = w_in.shape[0]
    n = bsz * seq
    assert seq % DIL_CHUNK == 0 and seq % (2 * TOKEN_TILE) == 0

    bf = lambda w: w.astype(BF16)
    w_main = bf(w_in[:, :, _rope_column_order()])
    w_vt = bf(jnp.swapaxes(w_in[:, :, 1024:1536], 1, 2))
    g1, u1, d1 = bf(ffn1_gate), bf(ffn1_up), bf(ffn1_down)
    g2, u2, d2 = bf(ffn2_gate), bf(ffn2_up), bf(ffn2_down)
    wo = bf(w_out)
    n1 = ffn1_norm[:, None, :]
    nm = mix_norm[:, None, :]
    n2 = ffn2_norm[:, None, :]
    dg = dil_gain[:, None, :]

    cos, sin = _rope_tables(_to_residue_major(positions, 0))
    h = _to_residue_major(x, 1).reshape(n, D_MODEL)
    for l in range(depth):
        lam_init = 0.8 - 0.6 * math.exp(-0.3 * l)
        h = _ffn_call(h, l, n1, g1, u1, d1)
        qk, dil, vt = _proj_call(h, l, nm, w_main, w_vt, cos, sin, seq)
        vt4 = vt.reshape(bsz, seq // TOKEN_TILE, N_DIFF_HEADS * VT_ROWS, TOKEN_TILE)
        lam_vecs = jnp.stack([lambda_q1[l], lambda_k1[l], lambda_q2[l], lambda_k2[l]])
        d_out = _diff_call(qk.reshape(bsz, seq, QK_WIDTH), vt4, lam_vecs, subln_gain[l][:, None], lam_init)
        a_mix = _dil_call(dil.reshape(bsz, seq, DIL_QKV_WIDTH))
        final = final_norm[None, :] if l == depth - 1 else None
        h = _ffn_call(h, l, n2, g2, u2, d2,
                      mix=(d_out.reshape(n, DIFF_WIDTH), a_mix.reshape(n, DIL_WIDTH), dg, wo),
                      final=final)
    return _from_residue_major(h.reshape(bsz, seq, D_MODEL), 1)
```

```python
import functools
import math

import numpy as np
import jax
import jax.numpy as jnp
from jax import lax
from jax.experimental import pallas as pl
from jax.experimental.pallas import tpu as pltpu

D_MODEL = 1024
HEAD_DIM = 64
N_DIFF_HEADS = 4
DIFF_WIDTH = N_DIFF_HEADS * 2 * HEAD_DIM
DIL_WIDTH = 8 * HEAD_DIM
DIL_BLOCK = 128
DIL_MAX = 16
D_FF = 2816
ROPE_THETA = 10000.0
EPS = 1e-6
SUBLN_EPS = 1e-5
NEG_INF = -1e30
Q_SCALE = HEAD_DIM ** -0.5 * math.log2(math.e)
LAZY_LIMIT = 2.0 ** 90

LANES = 128
SUBLANES = 8
MXU_WIDTH = 256
VMEM_LIMIT = 56 * 1024 * 1024

QK_WIDTH = 2 * DIFF_WIDTH
DIL_QKV_WIDTH = 3 * DIL_WIDTH
PROJ_WIDTH = QK_WIDTH + DIL_QKV_WIDTH
ROPE_WIDTH = 4 * 512

FF_CHUNK = 256
TOKEN_TILE = 512
VT_ROWS = LANES + 16
DIL_CHUNK = DIL_BLOCK * DIL_MAX

BF16 = jnp.bfloat16
F32 = jnp.float32
NT_DIMS = (((1,), (1,)), ((), ()))


def _rmsnorm(xf, g, eps):
    return xf * lax.rsqrt(jnp.mean(xf * xf, axis=-1, keepdims=True) + eps) * g


def _params(semantics):
    return pltpu.CompilerParams(dimension_semantics=semantics, vmem_limit_bytes=VMEM_LIMIT)


def _resident(block_shape, index_map):
    return pl.BlockSpec(block_shape, index_map, pipeline_mode=pl.Buffered(1))


def _true_pos(idx):
    return (idx // DIL_BLOCK) * DIL_BLOCK + DIL_MAX * (idx % SUBLANES) + (idx % DIL_BLOCK) // SUBLANES


def _to_residue_major(a, axis):
    shape = a.shape
    a = a.reshape(shape[:axis] + (shape[axis] // DIL_BLOCK, SUBLANES, DIL_MAX) + shape[axis + 1:])
    return jnp.swapaxes(a, axis + 1, axis + 2).reshape(shape)


def _from_residue_major(a, axis):
    shape = a.shape
    a = a.reshape(shape[:axis] + (shape[axis] // DIL_BLOCK, DIL_MAX, SUBLANES) + shape[axis + 1:])
    return jnp.swapaxes(a, axis + 1, axis + 2).reshape(shape)


def _rope_kernel(pos_ref, inv_ref, sign_ref, cos_ref, sin_ref):
    ang = pos_ref[...].astype(F32) * inv_ref[...]
    cos_ref[...] = jnp.cos(ang)
    sin_ref[...] = jnp.sin(ang) * sign_ref[...]


def _rope_tables(positions):
    seq = positions.shape[0]
    inv = 1.0 / (ROPE_THETA ** (jnp.arange(0, HEAD_DIM, 2, dtype=F32) / HEAD_DIM))
    inv_t = jnp.tile(inv, 4)[None, :]
    sign = jnp.concatenate([-jnp.ones((64,), F32), jnp.ones((64,), F32)])[None, :]
    rows = min(seq, 1024)
    return pl.pallas_call(
        _rope_kernel,
        out_shape=(jax.ShapeDtypeStruct((seq, LANES), F32),) * 2,
        grid=(seq // rows,),
        in_specs=[pl.BlockSpec((rows, 1), lambda i: (i, 0)),
                  pl.BlockSpec((1, LANES), lambda i: (0, 0)),
                  pl.BlockSpec((1, LANES), lambda i: (0, 0))],
        out_specs=(pl.BlockSpec((rows, LANES), lambda i: (i, 0)),) * 2,
        compiler_params=_params(("arbitrary",)),
        name="rope_tables",
    )(positions.reshape(seq, 1), inv_t, sign)


def _swiglu_residual(xf, g, wg_ref, wu_ref, wd_ref, h_ref):
    xn = _rmsnorm(xf, g, EPS).astype(BF16)
    for c in range(D_FF // FF_CHUNK):
        sl = slice(c * FF_CHUNK, (c + 1) * FF_CHUNK)
        a = jnp.dot(xn, wg_ref[:, sl], preferred_element_type=F32)
        u = jnp.dot(xn, wu_ref[:, sl], preferred_element_type=F32)
        h_ref[:, sl] = (a * jax.nn.sigmoid(a) * u).astype(BF16)
    y = jnp.dot(h_ref[...], wd_ref[...], preferred_element_type=F32)
    return xf + 0.5 * y


def _ffn_kernel(*refs, has_mix, has_final):
    refs = list(refs)
    x_ref = refs.pop(0)
    if has_mix:
        d_ref, a_ref, dg_ref, wo_ref = (refs.pop(0) for _ in range(4))
    g_ref, wg_ref, wu_ref, wd_ref = (refs.pop(0) for _ in range(4))
    if has_final:
        fin_ref = refs.pop(0)
    o_ref, h_ref = refs
    xf = x_ref[...]
    if has_mix:
        a_out = _rmsnorm(a_ref[...], dg_ref[...], EPS).astype(BF16)
        xf = xf + jnp.dot(d_ref[...], wo_ref[:DIFF_WIDTH, :], preferred_element_type=F32)
        xf = xf + jnp.dot(a_out, wo_ref[DIFF_WIDTH:, :], preferred_element_type=F32)
    out = _swiglu_residual(xf, g_ref[...], wg_ref, wu_ref, wd_ref, h_ref)
    if has_final:
        out = _rmsnorm(out, fin_ref[...], EPS)
    o_ref[...] = out


def _ffn_call(x2, layer, norm, wg, wu, wd, mix=None, final=None):
    n = x2.shape[0]
    tm = TOKEN_TILE
    row = lambda i: (i, 0)
    const2 = lambda i: (0, 0)
    lay3 = lambda i: (layer, 0, 0)
    args = [x2]
    specs = [pl.BlockSpec((tm, D_MODEL), row)]
    if mix is not None:
        d_out, a_mix, dil_gain, w_out = mix
        args += [d_out, a_mix, dil_gain, w_out]
        specs += [pl.BlockSpec((tm, DIFF_WIDTH), row), pl.BlockSpec((tm, DIL_WIDTH), row),
                  pl.BlockSpec((None, 1, DIL_WIDTH), lay3),
                  _resident((None, D_MODEL, D_MODEL), lay3)]
    args += [norm, wg, wu, wd]
    specs += [pl.BlockSpec((None, 1, D_MODEL), lay3),
              _resident((None, D_MODEL, D_FF), lay3),
              _resident((None, D_MODEL, D_FF), lay3),
              _resident((None, D_FF, D_MODEL), lay3)]
    if final is not None:
        args.append(final)
        specs.append(pl.BlockSpec((1, D_MODEL), const2))
    return pl.pallas_call(
        functools.partial(_ffn_kernel, has_mix=mix is not None, has_final=final is not None),
        out_shape=jax.ShapeDtypeStruct((n, D_MODEL), F32),
        grid=(n // tm,),
        in_specs=specs,
        out_specs=pl.BlockSpec((tm, D_MODEL), row),
        scratch_shapes=[pltpu.VMEM((tm, D_FF), BF16)],
        compiler_params=_params(("parallel",)),
        name="mix_ffn" if mix is not None else "ffn",
    )(*args)


def _proj_kernel(x_ref, g_ref, w_ref, wvt_ref, cos_ref, sin_ref, qk_ref, dil_ref, vt_ref):
    xn = _rmsnorm(x_ref[...], g_ref[...], EPS).astype(BF16)
    cos = cos_ref[...]
    sin = sin_ref[...]
    for c in range(PROJ_WIDTH // MXU_WIDTH):
        lo = c * MXU_WIDTH
        t = jnp.dot(xn, w_ref[:, lo:lo + MXU_WIDTH], preferred_element_type=F32)
        if lo < ROPE_WIDTH:
            scale = Q_SCALE if lo % 1024 < 512 else 1.0
            halves = []
            for half in range(MXU_WIDTH // LANES):
                th = t[:, half * LANES:(half + 1) * LANES]
                halves.append((th * cos + pltpu.roll(th, 64, 1) * sin) * scale)
            t = jnp.concatenate(halves, axis=1)
        if lo < QK_WIDTH:
            qk_ref[:, lo:lo + MXU_WIDTH] = t.astype(BF16)
        else:
            dil_ref[:, lo - QK_WIDTH:lo - QK_WIDTH + MXU_WIDTH] = t
    vt = lax.dot_general(wvt_ref[...], xn, NT_DIMS, preferred_element_type=F32)
    for h in range(N_DIFF_HEADS):
        vt_ref[h * VT_ROWS:h * VT_ROWS + LANES, :] = vt[h * LANES:(h + 1) * LANES].astype(BF16)
        vt_ref[h * VT_ROWS + LANES:(h + 1) * VT_ROWS, :] = jnp.ones((VT_ROWS - LANES, vt.shape[1]), BF16)


def _proj_call(x2, layer, norm, w_main, w_vt, cos, sin, seq):
    n = x2.shape[0]
    tm = TOKEN_TILE
    tiles_per_seq = seq // tm
    row = lambda i: (i, 0)
    lay3 = lambda i: (layer, 0, 0)
    pos = lambda i: (i % tiles_per_seq, 0)
    return pl.pallas_call(
        _proj_kernel,
        out_shape=(jax.ShapeDtypeStruct((n, QK_WIDTH), BF16),
                   jax.ShapeDtypeStruct((n, DIL_QKV_WIDTH), F32),
                   jax.ShapeDtypeStruct((n // tm, N_DIFF_HEADS * VT_ROWS, tm), BF16)),
        grid=(n // tm,),
        in_specs=[pl.BlockSpec((tm, D_MODEL), row),
                  pl.BlockSpec((None, 1, D_MODEL), lay3),
                  _resident((None, D_MODEL, PROJ_WIDTH), lay3),
                  _resident((None, DIFF_WIDTH, D_MODEL), lay3),
                  pl.BlockSpec((tm, LANES), pos),
                  pl.BlockSpec((tm, LANES), pos)],
        out_specs=(pl.BlockSpec((tm, QK_WIDTH), row),
                   pl.BlockSpec((tm, DIL_QKV_WIDTH), row),
                   pl.BlockSpec((None, N_DIFF_HEADS * VT_ROWS, tm), lambda i: (i, 0, 0))),
        compiler_params=_params(("parallel",)),
        name="proj_rope",
    )(x2, norm, w_main, w_vt, cos, sin)


def _diff_kernel(lam_ref, gain_ref, q_ref, k_ref, vt_ref, o_ref, qall_ref, acc_ref, *, lam_init, blk):
    i = pl.program_id(2)
    ncol = 4 * blk
    lane = lax.broadcasted_iota(jnp.int32, (blk, LANES), 1)
    is_first = (lane % 64) < 32
    for hh in range(2):
        q = q_ref[0, hh * blk:(hh + 1) * blk, :]
        zero = jnp.zeros_like(q)
        qall_ref[(2 * hh) * blk:(2 * hh + 1) * blk, :] = jnp.where(is_first, q, zero)
        qall_ref[(2 * hh + 1) * blk:(2 * hh + 2) * blk, :] = jnp.where(is_first, zero, q)

    def scores(j, nkb, col0, masked_cols):
        keys = nkb * blk
        kb = k_ref[0, pl.ds(pl.multiple_of(j * blk, blk), keys), :]
        s = lax.dot_general(kb, qall_ref[col0:, :], NT_DIMS, preferred_element_type=F32)
        if masked_cols:
            key_idx = lax.broadcasted_iota(jnp.int32, s.shape, 0)
            col_idx = lax.broadcasted_iota(jnp.int32, s.shape, 1)
            visible = (_true_pos(key_idx) <= _true_pos(col_idx % blk)) | (col_idx >= masked_cols)
            s = jnp.where(visible, s, NEG_INF)
        return s

    def values(j, nkb, p):
        return sum(jnp.dot(vt_ref[0, j + t], p[t * blk:(t + 1) * blk], preferred_element_type=F32)
                   for t in range(nkb))

    def exact_block(j, nkb, m_old, col0, masked_cols):
        s = scores(j, nkb, col0, masked_cols)
        m_prev = m_old[:, col0:]
        m_new = jnp.maximum(m_prev, jnp.max(s, axis=0, keepdims=True))
        alpha = jnp.exp2(m_prev - m_new)
        p = jnp.exp2(s - m_new).astype(BF16)
        acc_ref[:, col0:] = alpha * acc_ref[:, col0:] + values(j, nkb, p)
        if col0:
            m_new = jnp.concatenate([m_old[:, :col0], m_new], axis=1)
        return m_new

    def single_pass(j, nkb, shift, col0, masked_cols, first=False):
        s = scores(j, nkb, col0, masked_cols)
        if first:
            shift = s[0:1, :]
        p = jnp.exp2(s - shift[:, col0:]).astype(BF16)
        pv = values(j, nkb, p)
        if first:
            acc_ref[...] = pv
        else:
            acc_ref[:, col0:] += pv
        return shift

    def attend(exact):
        if exact:
            acc_ref[...] = jnp.zeros_like(acc_ref)
            m = jnp.full((1, ncol), NEG_INF, F32)
            m = exact_block(2 * i, 1, m, 0, 2 * blk)
            m = exact_block(2 * i + 1, 1, m, 2 * blk, 2 * blk)
            lax.fori_loop(0, i, lambda jj, mm: exact_block(2 * jj, 2, mm, 0, 0), m)
            return
        shift = single_pass(2 * i, 1, None, 0, 2 * blk, first=True)
        single_pass(2 * i + 1, 1, shift, 2 * blk, 2 * blk)

        def two_pairs(jj, carry):
            single_pass(4 * jj, 2, shift, 0, 0)
            single_pass(4 * jj + 2, 2, shift, 0, 0)
            return carry

        lax.fori_loop(0, i // 2, two_pairs, 0)

        @pl.when(i % 2 == 1)
        def _():
            single_pass(2 * (i - 1), 2, shift, 0, 0)

    lv = lam_ref[...]
    lam = (jnp.exp(jnp.sum(lv[0:1] * lv[1:2], axis=1, keepdims=True))
           - jnp.exp(jnp.sum(lv[2:3] * lv[3:4], axis=1, keepdims=True)) + lam_init)

    def finalize():
        mean_sq = []
        for hh in range(2):
            c0 = slice((2 * hh) * blk, (2 * hh + 1) * blk)
            c1 = slice((2 * hh + 1) * blk, (2 * hh + 2) * blk)
            o = (acc_ref[:LANES, c0] / acc_ref[LANES:LANES + 1, c0]
                 - lam * (acc_ref[:LANES, c1] / acc_ref[LANES:LANES + 1, c1]))
            mean_sq.append(jnp.mean(o * o, axis=0, keepdims=True))
            y = o * lax.rsqrt(mean_sq[-1] + SUBLN_EPS) * gain_ref[...] * (1.0 - lam_init)
            o_ref[0, hh * blk:(hh + 1) * blk, :] = y.T.astype(BF16)
        return mean_sq

    attend(exact=False)
    mean_sq = finalize()
    finite = jnp.sum((mean_sq[0] + mean_sq[1]) * 0.0) == 0.0
    stale = jnp.logical_not((jnp.max(acc_ref[LANES:LANES + 1, :]) <= LAZY_LIMIT) & finite)

    @pl.when(stale)
    def _():
        attend(exact=True)
        finalize()


def _diff_call(qk3, vt4, lam_vecs, gain_col, lam_init):
    bsz, seq, _ = qk3.shape
    blk = TOKEN_TILE
    nblk = seq // blk
    k_lane_block = DIFF_WIDTH // LANES
    return pl.pallas_call(
        functools.partial(_diff_kernel, lam_init=lam_init, blk=blk),
        out_shape=jax.ShapeDtypeStruct((bsz, seq, DIFF_WIDTH), BF16),
        grid=(bsz, N_DIFF_HEADS, nblk // 2),
        in_specs=[pl.BlockSpec((4, HEAD_DIM), lambda b, h, i: (0, 0)),
                  pl.BlockSpec((LANES, 1), lambda b, h, i: (0, 0)),
                  pl.BlockSpec((1, 2 * blk, LANES), lambda b, h, i: (b, i, h)),
                  pl.BlockSpec((1, seq, LANES), lambda b, h, i: (b, 0, k_lane_block + h)),
                  pl.BlockSpec((1, nblk, VT_ROWS, blk), lambda b, h, i: (b, 0, h, 0))],
        out_specs=pl.BlockSpec((1, 2 * blk, LANES), lambda b, h, i: (b, i, h)),
        scratch_shapes=[pltpu.VMEM((4 * blk, LANES), BF16), pltpu.VMEM((VT_ROWS, 4 * blk), F32)],
        compiler_params=_params(("parallel", "parallel", "arbitrary")),
        name="diff_attn",
    )(lam_vecs, gain_col, qk3, qk3, vt4)


def _dil_kernel(q_ref, k_ref, v_ref, o_ref, m_ref, l_ref, acc_ref):
    c = pl.program_id(2)
    base = c * DIL_CHUNK
    blk = DIL_BLOCK
    lane = lax.broadcasted_iota(jnp.int32, (blk, LANES), 1)
    v_first = lane < 64
    lane2 = lax.broadcasted_iota(jnp.int32, (2 * blk, LANES), 1)
    row2 = lax.broadcasted_iota(jnp.int32, (2 * blk, LANES), 0)
    q_keep = ((lane2 % 64) < 32) == (row2 < blk)
    qrow = lax.broadcasted_iota(jnp.int32, (2 * blk, 2 * blk), 0) % blk
    kcol = lax.broadcasted_iota(jnp.int32, (2 * blk, 2 * blk), 1)

    def band(qpos, kpos):
        dist = qpos + blk - kpos
        return (dist >= 0) & (dist <= blk)

    def score_cap(visible):
        return jnp.where(visible, jnp.inf, NEG_INF).astype(F32)

    pos4 = lambda idx: 32 * (idx // 32) + 4 * (idx % SUBLANES) + (idx % 32) // SUBLANES
    band1 = band(_true_pos(qrow), _true_pos(kcol))
    band4 = band(pos4(qrow), blk * (kcol // blk) + pos4(kcol % blk))
    band16 = band(qrow, kcol)

    def strip(start):
        return pl.ds(start if isinstance(start, int) else pl.multiple_of(start, SUBLANES), SUBLANES)

    def strips(ref, starts):
        return jnp.concatenate([ref[0, strip(s), :] for s in starts], axis=0)

    def attend(q, k, v, cap):
        qb = q.astype(BF16)
        q2 = jnp.concatenate([qb, qb], axis=0)
        q2 = jnp.where(q_keep, q2, jnp.zeros_like(q2))
        s = lax.dot_general(q2, k, NT_DIMS, preferred_element_type=F32)
        s = jnp.minimum(s, cap)
        m = jnp.max(s, axis=1, keepdims=True)
        p = jnp.exp2(s - m).astype(BF16)
        v_ones = jnp.concatenate([v, jnp.ones_like(v)], axis=1)
        num = jnp.dot(p, v_ones, preferred_element_type=F32)
        m_pair = jnp.where(v_first, jnp.broadcast_to(m[:blk], (blk, LANES)), jnp.broadcast_to(m[blk:], (blk, LANES)))
        return (m_pair, jnp.where(v_first, num[:blk, LANES:], num[blk:, LANES:]),
                jnp.where(v_first, num[:blk, :LANES], num[blk:, :LANES]))

    def merge(q_starts, m, den, num):
        for t, s0 in enumerate(q_starts):
            rows = strip(s0)
            sl = slice(t * SUBLANES, (t + 1) * SUBLANES)
            m_st = m_ref[rows, :]
            m_new = jnp.maximum(m_st, m[sl])
            w_st = jnp.exp2(m_st - m_new)
            w_new = jnp.exp2(m[sl] - m_new)
            m_ref[rows, :] = m_new
            l_ref[rows, :] = w_st * l_ref[rows, :] + w_new * den[sl]
            acc_ref[rows, :] = w_st * acc_ref[rows, :] + w_new * num[sl]

    first_key = jnp.where(c > 0, 0, blk)

    cap_first = lambda in_band: score_cap(in_band & (kcol >= first_key))
    both = lambda prev, own: jnp.concatenate([prev, own], axis=0)
    gather = lambda ref, starts: strips(ref, starts).astype(BF16)

    cap_a, cap_b = cap_first(band1), score_cap(band1)
    start0 = jnp.maximum(base - blk, 0)
    load = lambda ref, start: ref[0, pl.ds(pl.multiple_of(start, blk), blk), :].astype(BF16)
    k_prev, v_prev = load(k_ref, start0), load(v_ref, start0)
    for u in range(DIL_MAX):
        rows = slice(u * blk, (u + 1) * blk)
        k_own, v_own = load(k_ref, base + u * blk), load(v_ref, base + u * blk)
        m, den, num = attend(q_ref[0, rows, :], both(k_prev, k_own), both(v_prev, v_own), cap_a if u == 0 else cap_b)
        m_ref[rows, :] = m
        l_ref[rows, :] = den
        acc_ref[rows, :] = num
        k_prev, v_prev = k_own, v_own

    cap_a, cap_b = cap_first(band4), score_cap(band4)
    start0 = jnp.maximum(base - 512, 0)
    for r4 in range(4):
        block_rows = lambda bq: [blk * (4 * bq + ul) + SUBLANES * (4 * a + r4) for ul in range(4) for a in range(4)]
        prev = [start0 + s for s in block_rows(0)]
        k_prev, v_prev = gather(k_ref, prev), gather(v_ref, prev)
        for bq in range(4):
            rel = block_rows(bq)
            own = [base + s for s in rel]
            k_own, v_own = gather(k_ref, own), gather(v_ref, own)
            m, den, num = attend(strips(q_ref, rel), both(k_prev, k_own), both(v_prev, v_own),
                                 cap_a if bq == 0 else cap_b)
            merge(rel, m, den, num)
            k_prev, v_prev = k_own, v_own

    cap_a = cap_first(band16)
    start0 = jnp.maximum(base - DIL_CHUNK, 0)
    for r in range(DIL_MAX):
        rel = [blk * u + SUBLANES * r for u in range(DIL_MAX)]
        k = both(gather(k_ref, [start0 + s for s in rel]), gather(k_ref, [base + s for s in rel]))
        v = both(gather(v_ref, [start0 + s for s in rel]), gather(v_ref, [base + s for s in rel]))
        m, den, num = attend(strips(q_ref, rel), k, v, cap_a)
        merge(rel, m, den, num)

    o_ref[0] = acc_ref[...] / l_ref[...]


def _dil_call(dil3):
    bsz, seq, _ = dil3.shape
    pairs = DIL_WIDTH // LANES
    return pl.pallas_call(
        _dil_kernel,
        out_shape=jax.ShapeDtypeStruct((bsz, seq, DIL_WIDTH), F32),
        grid=(bsz, pairs, seq // DIL_CHUNK),
        in_specs=[pl.BlockSpec((1, DIL_CHUNK, LANES), lambda b, g, c: (b, c, g)),
                  pl.BlockSpec((1, seq, LANES), lambda b, g, c: (b, 0, pairs + g)),
                  pl.BlockSpec((1, seq, LANES), lambda b, g, c: (b, 0, 2 * pairs + g))],
        out_specs=pl.BlockSpec((1, DIL_CHUNK, LANES), lambda b, g, c: (b, c, g)),
        scratch_shapes=[pltpu.VMEM((DIL_CHUNK, LANES), F32)] * 3,
        compiler_params=_params(("parallel", "parallel", "arbitrary")),
        name="dilated",
    )(dil3, dil3, dil3)


def _rope_column_order():
    half = HEAD_DIM // 2
    cols = []
    for start in (0, 512, 1536, 2048):
        for g in range(512 // LANES):
            a = start + g * LANES
            b = a + HEAD_DIM
            cols += list(range(a, a + half)) + list(range(b, b + half))
            cols += list(range(a + half, a + HEAD_DIM)) + list(range(b + half, b + HEAD_DIM))
    cols += list(range(2560, 3072))
    return np.asarray(cols, dtype=np.int32)


def kernel(x, positions, ffn1_norm, ffn1_gate, ffn1_up, ffn1_down, mix_norm, w_in, lambda_q1, lambda_k1, lambda_q2, lambda_k2, subln_gain, dil_gain, w_out, ffn2_norm, ffn2_gate, ffn2_up, ffn2_down, final_norm):
    bsz, seq, _ = x.shape
    depth = w_in.shape[0]
    n = bsz * seq
    assert seq % DIL_CHUNK == 0 and seq % (2 * TOKEN_TILE) == 0

    bf = lambda w: w.astype(BF16)
    w_main = bf(w_in[:, :, _rope_column_order()])
    w_vt = bf(jnp.swapaxes(w_in[:, :, 1024:1536], 1, 2))
    g1, u1, d1 = bf(ffn1_gate), bf(ffn1_up), bf(ffn1_down)
    g2, u2, d2 = bf(ffn2_gate), bf(ffn2_up), bf(ffn2_down)
    wo = bf(w_out)
    n1 = ffn1_norm[:, None, :]
    nm = mix_norm[:, None, :]
    n2 = ffn2_norm[:, None, :]
    dg = dil_gain[:, None, :]

    cos, sin = _rope_tables(_to_residue_major(positions, 0))
    h = _to_residue_major(x, 1).reshape(n, D_MODEL)
    for l in range(depth):
        lam_init = 0.8 - 0.6 * math.exp(-0.3 * l)
        h = _ffn_call(h, l, n1, g1, u1, d1)
        qk, dil, vt = _proj_call(h, l, nm, w_main, w_vt, cos, sin, seq)
        vt4 = vt.reshape(bsz, seq // TOKEN_TILE, N_DIFF_HEADS * VT_ROWS, TOKEN_TILE)
        lam_vecs = jnp.stack([lambda_q1[l], lambda_k1[l], lambda_q2[l], lambda_k2[l]])
        d_out = _diff_call(qk.reshape(bsz, seq, QK_WIDTH), vt4, lam_vecs, subln_gain[l][:, None], lam_init)
        a_mix = _dil_call(dil.reshape(bsz, seq, DIL_QKV_WIDTH))
        final = final_norm[None, :] if l == depth - 1 else None
        h = _ffn_call(h, l, n2, g2, u2, d2,
                      mix=(d_out.reshape(n, DIFF_WIDTH), a_mix.reshape(n, DIL_WIDTH), dg, wo),
                      final=final)
    return _from_residue_major(h.reshape(bsz, seq, D_MODEL), 1)
```

```python
import functools
import math

import numpy as np
import jax
import jax.numpy as jnp
from jax import lax
from jax.experimental import pallas as pl
from jax.experimental.pallas import tpu as pltpu

D_MODEL = 1024
HEAD_DIM = 64
N_DIFF_HEADS = 4
DIFF_WIDTH = N_DIFF_HEADS * 2 * HEAD_DIM
DIL_WIDTH = 8 * HEAD_DIM
DIL_BLOCK = 128
DIL_MAX = 16
D_FF = 2816
ROPE_THETA = 10000.0
EPS = 1e-6
SUBLN_EPS = 1e-5
NEG_INF = -1e30
Q_SCALE = HEAD_DIM ** -0.5 * math.log2(math.e)
LAZY_LIMIT = 2.0 ** 90

LANES = 128
SUBLANES = 8
MXU_WIDTH = 256
VMEM_LIMIT = 56 * 1024 * 1024

QK_WIDTH = 2 * DIFF_WIDTH
DIL_QKV_WIDTH = 3 * DIL_WIDTH
PROJ_WIDTH = QK_WIDTH + DIL_QKV_WIDTH
ROPE_WIDTH = 4 * 512

FF_CHUNK = 256
TOKEN_TILE = 512
FFN_TILE = 1024
VT_ROWS = LANES + 16
DIL_CHUNK = DIL_BLOCK * DIL_MAX

BF16 = jnp.bfloat16
F32 = jnp.float32
NT_DIMS = (((1,), (1,)), ((), ()))


def _rmsnorm(xf, g, eps):
    return xf * lax.rsqrt(jnp.mean(xf * xf, axis=-1, keepdims=True) + eps) * g


def _params(semantics):
    return pltpu.CompilerParams(dimension_semantics=semantics, vmem_limit_bytes=VMEM_LIMIT)


def _resident(block_shape, index_map):
    return pl.BlockSpec(block_shape, index_map, pipeline_mode=pl.Buffered(1))


def _true_pos(idx):
    return (idx // DIL_BLOCK) * DIL_BLOCK + DIL_MAX * (idx % SUBLANES) + (idx % DIL_BLOCK) // SUBLANES


def _to_residue_major(a, axis):
    shape = a.shape
    a = a.reshape(shape[:axis] + (shape[axis] // DIL_BLOCK, SUBLANES, DIL_MAX) + shape[axis + 1:])
    return jnp.swapaxes(a, axis + 1, axis + 2).reshape(shape)


def _from_residue_major(a, axis):
    shape = a.shape
    a = a.reshape(shape[:axis] + (shape[axis] // DIL_BLOCK, DIL_MAX, SUBLANES) + shape[axis + 1:])
    return jnp.swapaxes(a, axis + 1, axis + 2).reshape(shape)


def _rope_kernel(pos_ref, inv_ref, sign_ref, cos_ref, sin_ref):
    ang = pos_ref[...].astype(F32) * inv_ref[...]
    cos_ref[...] = jnp.cos(ang)
    sin_ref[...] = jnp.sin(ang) * sign_ref[...]


def _rope_tables(positions):
    seq = positions.shape[0]
    inv = 1.0 / (ROPE_THETA ** (jnp.arange(0, HEAD_DIM, 2, dtype=F32) / HEAD_DIM))
    inv_t = jnp.tile(inv, 4)[None, :]
    sign = jnp.concatenate([-jnp.ones((64,), F32), jnp.ones((64,), F32)])[None, :]
    rows = min(seq, 1024)
    return pl.pallas_call(
        _rope_kernel,
        out_shape=(jax.ShapeDtypeStruct((seq, LANES), F32),) * 2,
        grid=(seq // rows,),
        in_specs=[pl.BlockSpec((rows, 1), lambda i: (i, 0)),
                  pl.BlockSpec((1, LANES), lambda i: (0, 0)),
                  pl.BlockSpec((1, LANES), lambda i: (0, 0))],
        out_specs=(pl.BlockSpec((rows, LANES), lambda i: (i, 0)),) * 2,
        compiler_params=_params(("arbitrary",)),
        name="rope_tables",
    )(positions.reshape(seq, 1), inv_t, sign)


def _swiglu_residual(xf, g, wg_ref, wu_ref, wd_ref, h_ref):
    xn = _rmsnorm(xf, g, EPS).astype(BF16)
    for c in range(D_FF // FF_CHUNK):
        sl = slice(c * FF_CHUNK, (c + 1) * FF_CHUNK)
        a = jnp.dot(xn, wg_ref[:, sl], preferred_element_type=F32)
        u = jnp.dot(xn, wu_ref[:, sl], preferred_element_type=F32)
        h_ref[:, sl] = (a * jax.nn.sigmoid(a) * u).astype(BF16)
    y = jnp.dot(h_ref[...], wd_ref[...], preferred_element_type=F32)
    return xf + 0.5 * y


def _ffn_kernel(*refs, has_mix, has_final):
    refs = list(refs)
    x_ref = refs.pop(0)
    if has_mix:
        d_ref, a_ref, dg_ref, wo_ref = (refs.pop(0) for _ in range(4))
    g_ref, wg_ref, wu_ref, wd_ref = (refs.pop(0) for _ in range(4))
    if has_final:
        fin_ref = refs.pop(0)
    o_ref, h_ref = refs
    xf = x_ref[...]
    if has_mix:
        a_out = _rmsnorm(a_ref[...], dg_ref[...], EPS).astype(BF16)
        xf = xf + jnp.dot(d_ref[...], wo_ref[:DIFF_WIDTH, :], preferred_element_type=F32)
        xf = xf + jnp.dot(a_out, wo_ref[DIFF_WIDTH:, :], preferred_element_type=F32)
    out = _swiglu_residual(xf, g_ref[...], wg_ref, wu_ref, wd_ref, h_ref)
    if has_final:
        out = _rmsnorm(out, fin_ref[...], EPS)
    o_ref[...] = out


def _ffn_call(x2, layer, norm, wg, wu, wd, mix=None, final=None):
    n = x2.shape[0]
    tm = FFN_TILE
    row = lambda i: (i, 0)
    const2 = lambda i: (0, 0)
    lay3 = lambda i: (layer, 0, 0)
    args = [x2]
    specs = [pl.BlockSpec((tm, D_MODEL), row)]
    if mix is not None:
        d_out, a_mix, dil_gain, w_out = mix
        args += [d_out, a_mix, dil_gain, w_out]
        specs += [pl.BlockSpec((tm, DIFF_WIDTH), row), pl.BlockSpec((tm, DIL_WIDTH), row),
                  pl.BlockSpec((None, 1, DIL_WIDTH), lay3),
                  _resident((None, D_MODEL, D_MODEL), lay3)]
    args += [norm, wg, wu, wd]
    specs += [pl.BlockSpec((None, 1, D_MODEL), lay3),
              _resident((None, D_MODEL, D_FF), lay3),
              _resident((None, D_MODEL, D_FF), lay3),
              _resident((None, D_FF, D_MODEL), lay3)]
    if final is not None:
        args.append(final)
        specs.append(pl.BlockSpec((1, D_MODEL), const2))
    return pl.pallas_call(
        functools.partial(_ffn_kernel, has_mix=mix is not None, has_final=final is not None),
        out_shape=jax.ShapeDtypeStruct((n, D_MODEL), F32),
        grid=(n // tm,),
        in_specs=specs,
        out_specs=pl.BlockSpec((tm, D_MODEL), row),
        scratch_shapes=[pltpu.VMEM((tm, D_FF), BF16)],
        compiler_params=_params(("parallel",)),
        name="mix_ffn" if mix is not None else "ffn",
    )(*args)


def _proj_kernel(x_ref, g_ref, w_ref, wvt_ref, cos_ref, sin_ref, qk_ref, dil_ref, vt_ref):
    xn = _rmsnorm(x_ref[...], g_ref[...], EPS).astype(BF16)
    cos = cos_ref[...]
    sin = sin_ref[...]
    for c in range(PROJ_WIDTH // MXU_WIDTH):
        lo = c * MXU_WIDTH
        t = jnp.dot(xn, w_ref[:, lo:lo + MXU_WIDTH], preferred_element_type=F32)
        if lo < ROPE_WIDTH:
            scale = Q_SCALE if lo % 1024 < 512 else 1.0
            halves = []
            for half in range(MXU_WIDTH // LANES):
                th = t[:, half * LANES:(half + 1) * LANES]
                halves.append((th * cos + pltpu.roll(th, 64, 1) * sin) * scale)
            t = jnp.concatenate(halves, axis=1)
        if lo < QK_WIDTH:
            qk_ref[:, lo:lo + MXU_WIDTH] = t.astype(BF16)
        else:
            dil_ref[:, lo - QK_WIDTH:lo - QK_WIDTH + MXU_WIDTH] = t
    vt = lax.dot_general(wvt_ref[...], xn, NT_DIMS, preferred_element_type=F32)
    for h in range(N_DIFF_HEADS):
        vt_ref[h * VT_ROWS:h * VT_ROWS + LANES, :] = vt[h * LANES:(h + 1) * LANES].astype(BF16)
        vt_ref[h * VT_ROWS + LANES:(h + 1) * VT_ROWS, :] = jnp.ones((VT_ROWS - LANES, vt.shape[1]), BF16)


def _proj_call(x2, layer, norm, w_main, w_vt, cos, sin, seq):
    n = x2.shape[0]
    tm = TOKEN_TILE
    tiles_per_seq = seq // tm
    row = lambda i: (i, 0)
    lay3 = lambda i: (layer, 0, 0)
    pos = lambda i: (i % tiles_per_seq, 0)
    return pl.pallas_call(
        _proj_kernel,
        out_shape=(jax.ShapeDtypeStruct((n, QK_WIDTH), BF16),
                   jax.ShapeDtypeStruct((n, DIL_QKV_WIDTH), F32),
                   jax.ShapeDtypeStruct((n // tm, N_DIFF_HEADS * VT_ROWS, tm), BF16)),
        grid=(n // tm,),
        in_specs=[pl.BlockSpec((tm, D_MODEL), row),
                  pl.BlockSpec((None, 1, D_MODEL), lay3),
                  _resident((None, D_MODEL, PROJ_WIDTH), lay3),
                  _resident((None, DIFF_WIDTH, D_MODEL), lay3),
                  pl.BlockSpec((tm, LANES), pos),
                  pl.BlockSpec((tm, LANES), pos)],
        out_specs=(pl.BlockSpec((tm, QK_WIDTH), row),
                   pl.BlockSpec((tm, DIL_QKV_WIDTH), row),
                   pl.BlockSpec((None, N_DIFF_HEADS * VT_ROWS, tm), lambda i: (i, 0, 0))),
        compiler_params=_params(("parallel",)),
        name="proj_rope",
    )(x2, norm, w_main, w_vt, cos, sin)


def _diff_kernel(lam_ref, gain_ref, q_ref, k_ref, vt_ref, o_ref, qall_ref, acc_ref, *, lam_init, blk):
    i = pl.program_id(2)
    ncol = 4 * blk
    lane = lax.broadcasted_iota(jnp.int32, (blk, LANES), 1)
    is_first = (lane % 64) < 32
    for hh in range(2):
        q = q_ref[0, hh * blk:(hh + 1) * blk, :]
        zero = jnp.zeros_like(q)
        qall_ref[(2 * hh) * blk:(2 * hh + 1) * blk, :] = jnp.where(is_first, q, zero)
        qall_ref[(2 * hh + 1) * blk:(2 * hh + 2) * blk, :] = jnp.where(is_first, zero, q)

    def scores(j, nkb, col0, masked_cols):
        keys = nkb * blk
        kb = k_ref[0, pl.ds(pl.multiple_of(j * blk, blk), keys), :]
        s = lax.dot_general(kb, qall_ref[col0:, :], NT_DIMS, preferred_element_type=F32)
        if masked_cols:
            key_idx = lax.broadcasted_iota(jnp.int32, s.shape, 0)
            col_idx = lax.broadcasted_iota(jnp.int32, s.shape, 1)
            visible = (_true_pos(key_idx) <= _true_pos(col_idx % blk)) | (col_idx >= masked_cols)
            s = jnp.where(visible, s, NEG_INF)
        return s

    def values(j, nkb, p):
        return sum(jnp.dot(vt_ref[0, j + t], p[t * blk:(t + 1) * blk], preferred_element_type=F32)
                   for t in range(nkb))

    def exact_block(j, nkb, m_old, col0, masked_cols):
        s = scores(j, nkb, col0, masked_cols)
        m_prev = m_old[:, col0:]
        m_new = jnp.maximum(m_prev, jnp.max(s, axis=0, keepdims=True))
        alpha = jnp.exp2(m_prev - m_new)
        p = jnp.exp2(s - m_new).astype(BF16)
        acc_ref[:, col0:] = alpha * acc_ref[:, col0:] + values(j, nkb, p)
        if col0:
            m_new = jnp.concatenate([m_old[:, :col0], m_new], axis=1)
        return m_new

    def single_pass(j, nkb, shift, col0, masked_cols, first=False):
        s = scores(j, nkb, col0, masked_cols)
        if first:
            shift = s[0:1, :]
        p = jnp.exp2(s - shift[:, col0:]).astype(BF16)
        pv = values(j, nkb, p)
        if first:
            acc_ref[...] = pv
        else:
            acc_ref[:, col0:] += pv
        return shift

    def attend(exact):
        if exact:
            acc_ref[...] = jnp.zeros_like(acc_ref)
            m = jnp.full((1, ncol), NEG_INF, F32)
            m = exact_block(2 * i, 1, m, 0, 2 * blk)
            m = exact_block(2 * i + 1, 1, m, 2 * blk, 2 * blk)
            lax.fori_loop(0, i, lambda jj, mm: exact_block(2 * jj, 2, mm, 0, 0), m)
            return
        shift = single_pass(2 * i, 1, None, 0, 2 * blk, first=True)
        single_pass(2 * i + 1, 1, shift, 2 * blk, 2 * blk)

        def two_pairs(jj, carry):
            single_pass(4 * jj, 2, shift, 0, 0)
            single_pass(4 * jj + 2, 2, shift, 0, 0)
            return carry

        lax.fori_loop(0, i // 2, two_pairs, 0)

        @pl.when(i % 2 == 1)
        def _():
            single_pass(2 * (i - 1), 2, shift, 0, 0)

    lv = lam_ref[...]
    lam = (jnp.exp(jnp.sum(lv[0:1] * lv[1:2], axis=1, keepdims=True))
           - jnp.exp(jnp.sum(lv[2:3] * lv[3:4], axis=1, keepdims=True)) + lam_init)

    def finalize():
        mean_sq = []
        for hh in range(2):
            c0 = slice((2 * hh) * blk, (2 * hh + 1) * blk)
            c1 = slice((2 * hh + 1) * blk, (2 * hh + 2) * blk)
            inv0 = 1.0 / acc_ref[LANES:LANES + 1, c0]
            inv1 = lam / acc_ref[LANES:LANES + 1, c1]
            o = acc_ref[:LANES, c0] * inv0 - acc_ref[:LANES, c1] * inv1
            mean_sq.append(jnp.mean(o * o, axis=0, keepdims=True))
            y = o * lax.rsqrt(mean_sq[-1] + SUBLN_EPS) * gain_ref[...] * (1.0 - lam_init)
            o_ref[0, hh * blk:(hh + 1) * blk, :] = y.T.astype(BF16)
        return mean_sq

    attend(exact=False)
    mean_sq = finalize()
    finite = jnp.sum((mean_sq[0] + mean_sq[1]) * 0.0) == 0.0
    stale = jnp.logical_not((jnp.max(acc_ref[LANES:LANES + 1, :]) <= LAZY_LIMIT) & finite)

    @pl.when(stale)
    def _():
        attend(exact=True)
        finalize()


def _diff_call(qk3, vt4, lam_vecs, gain_col, lam_init):
    bsz, seq, _ = qk3.shape
    blk = TOKEN_TILE
    nblk = seq // blk
    k_lane_block = DIFF_WIDTH // LANES
    return pl.pallas_call(
        functools.partial(_diff_kernel, lam_init=lam_init, blk=blk),
        out_shape=jax.ShapeDtypeStruct((bsz, seq, DIFF_WIDTH), BF16),
        grid=(bsz, N_DIFF_HEADS, nblk // 2),
        in_specs=[pl.BlockSpec((4, HEAD_DIM), lambda b, h, i: (0, 0)),
                  pl.BlockSpec((LANES, 1), lambda b, h, i: (0, 0)),
                  pl.BlockSpec((1, 2 * blk, LANES), lambda b, h, i: (b, i, h)),
                  pl.BlockSpec((1, seq, LANES), lambda b, h, i: (b, 0, k_lane_block + h)),
                  pl.BlockSpec((1, nblk, VT_ROWS, blk), lambda b, h, i: (b, 0, h, 0))],
        out_specs=pl.BlockSpec((1, 2 * blk, LANES), lambda b, h, i: (b, i, h)),
        scratch_shapes=[pltpu.VMEM((4 * blk, LANES), BF16), pltpu.VMEM((VT_ROWS, 4 * blk), F32)],
        compiler_params=_params(("parallel", "parallel", "arbitrary")),
        name="diff_attn",
    )(lam_vecs, gain_col, qk3, qk3, vt4)


def _dil_kernel(q_ref, k_ref, v_ref, o_ref, m_ref, l_ref, acc_ref):
    c = pl.program_id(2)
    base = c * DIL_CHUNK
    blk = DIL_BLOCK
    lane = lax.broadcasted_iota(jnp.int32, (blk, LANES), 1)
    v_first = lane < 64
    lane2 = lax.broadcasted_iota(jnp.int32, (2 * blk, LANES), 1)
    row2 = lax.broadcasted_iota(jnp.int32, (2 * blk, LANES), 0)
    q_keep = ((lane2 % 64) < 32) == (row2 < blk)
    qrow = lax.broadcasted_iota(jnp.int32, (2 * blk, 2 * blk), 0) % blk
    kcol = lax.broadcasted_iota(jnp.int32, (2 * blk, 2 * blk), 1)

    def band(qpos, kpos):
        dist = qpos + blk - kpos
        return (dist >= 0) & (dist <= blk)

    def score_cap(visible):
        return jnp.where(visible, jnp.inf, NEG_INF).astype(F32)

    pos4 = lambda idx: 32 * (idx // 32) + 4 * (idx % SUBLANES) + (idx % 32) // SUBLANES
    band1 = band(_true_pos(qrow), _true_pos(kcol))
    band4 = band(pos4(qrow), blk * (kcol // blk) + pos4(kcol % blk))
    band16 = band(qrow, kcol)

    def strip(start):
        return pl.ds(start if isinstance(start, int) else pl.multiple_of(start, SUBLANES), SUBLANES)

    def strips(ref, starts):
        return jnp.concatenate([ref[0, strip(s), :] for s in starts], axis=0)

    def attend(q, k, v, cap):
        qb = q.astype(BF16)
        q2 = jnp.concatenate([qb, qb], axis=0)
        q2 = jnp.where(q_keep, q2, jnp.zeros_like(q2))
        s = lax.dot_general(q2, k, NT_DIMS, preferred_element_type=F32)
        s = jnp.minimum(s, cap)
        m = jnp.max(s, axis=1, keepdims=True)
        p = jnp.exp2(s - m).astype(BF16)
        v_ones = jnp.concatenate([v, jnp.ones_like(v)], axis=1)
        num = jnp.dot(p, v_ones, preferred_element_type=F32)
        m_pair = jnp.where(v_first, jnp.broadcast_to(m[:blk], (blk, LANES)), jnp.broadcast_to(m[blk:], (blk, LANES)))
        return (m_pair, jnp.where(v_first, num[:blk, LANES:], num[blk:, LANES:]),
                jnp.where(v_first, num[:blk, :LANES], num[blk:, :LANES]))

    def merge(q_starts, m, den, num):
        for t, s0 in enumerate(q_starts):
            rows = strip(s0)
            sl = slice(t * SUBLANES, (t + 1) * SUBLANES)
            m_st = m_ref[rows, :]
            m_new = jnp.maximum(m_st, m[sl])
            w_st = jnp.exp2(m_st - m_new)
            w_new = jnp.exp2(m[sl] - m_new)
            m_ref[rows, :] = m_new
            l_ref[rows, :] = w_st * l_ref[rows, :] + w_new * den[sl]
            acc_ref[rows, :] = w_st * acc_ref[rows, :] + w_new * num[sl]

    first_key = jnp.where(c > 0, 0, blk)

    cap_first = lambda in_band: score_cap(in_band & (kcol >= first_key))
    both = lambda prev, own: jnp.concatenate([prev, own], axis=0)
    gather = lambda ref, starts: strips(ref, starts).astype(BF16)

    cap_a, cap_b = cap_first(band1), score_cap(band1)
    start0 = jnp.maximum(base - blk, 0)
    load = lambda ref, start: ref[0, pl.ds(pl.multiple_of(start, blk), blk), :].astype(BF16)
    k_prev, v_prev = load(k_ref, start0), load(v_ref, start0)
    for u in range(DIL_MAX):
        rows = slice(u * blk, (u + 1) * blk)
        k_own, v_own = load(k_ref, base + u * blk), load(v_ref, base + u * blk)
        m, den, num = attend(q_ref[0, rows, :], both(k_prev, k_own), both(v_prev, v_own), cap_a if u == 0 else cap_b)
        m_ref[rows, :] = m
        l_ref[rows, :] = den
        acc_ref[rows, :] = num
        k_prev, v_prev = k_own, v_own

    cap_a, cap_b = cap_first(band4), score_cap(band4)
    start0 = jnp.maximum(base - 512, 0)
    for r4 in range(4):
        block_rows = lambda bq: [blk * (4 * bq + ul) + SUBLANES * (4 * a + r4) for ul in range(4) for a in range(4)]
        prev = [start0 + s for s in block_rows(0)]
        k_prev, v_prev = gather(k_ref, prev), gather(v_ref, prev)
        for bq in range(4):
            rel = block_rows(bq)
            own = [base + s for s in rel]
            k_own, v_own = gather(k_ref, own), gather(v_ref, own)
            m, den, num = attend(strips(q_ref, rel), both(k_prev, k_own), both(v_prev, v_own),
                                 cap_a if bq == 0 else cap_b)
            merge(rel, m, den, num)
            k_prev, v_prev = k_own, v_own

    cap_a = cap_first(band16)
    start0 = jnp.maximum(base - DIL_CHUNK, 0)
    for r in range(DIL_MAX):
        rel = [blk * u + SUBLANES * r for u in range(DIL_MAX)]
        k = both(gather(k_ref, [start0 + s for s in rel]), gather(k_ref, [base + s for s in rel]))
        v = both(gather(v_ref, [start0 + s for s in rel]), gather(v_ref, [base + s for s in rel]))
        m, den, num = attend(strips(q_ref, rel), k, v, cap_a)
        merge(rel, m, den, num)

    o_ref[0] = acc_ref[...] / l_ref[...]


def _dil_call(dil3):
    bsz, seq, _ = dil3.shape
    pairs = DIL_WIDTH // LANES
    return pl.pallas_call(
        _dil_kernel,
        out_shape=jax.ShapeDtypeStruct((bsz, seq, DIL_WIDTH), F32),
        grid=(bsz, pairs, seq // DIL_CHUNK),
        in_specs=[pl.BlockSpec((1, DIL_CHUNK, LANES), lambda b, g, c: (b, c, g)),
                  pl.BlockSpec((1, seq, LANES), lambda b, g, c: (b, 0, pairs + g)),
                  pl.BlockSpec((1, seq, LANES), lambda b, g, c: (b, 0, 2 * pairs + g))],
        out_specs=pl.BlockSpec((1, DIL_CHUNK, LANES), lambda b, g, c: (b, c, g)),
        scratch_shapes=[pltpu.VMEM((DIL_CHUNK, LANES), F32)] * 3,
        compiler_params=_params(("parallel", "parallel", "arbitrary")),
        name="dilated",
    )(dil3, dil3, dil3)


def _rope_column_order():
    half = HEAD_DIM // 2
    cols = []
    for start in (0, 512, 1536, 2048):
        for g in range(512 // LANES):
            a = start + g * LANES
            b = a + HEAD_DIM
            cols += list(range(a, a + half)) + list(range(b, b + half))
            cols += list(range(a + half, a + HEAD_DIM)) + list(range(b + half, b + HEAD_DIM))
    cols += list(range(2560, 3072))
    return np.asarray(cols, dtype=np.int32)


def kernel(x, positions, ffn1_norm, ffn1_gate, ffn1_up, ffn1_down, mix_norm, w_in, lambda_q1, lambda_k1, lambda_q2, lambda_k2, subln_gain, dil_gain, w_out, ffn2_norm, ffn2_gate, ffn2_up, ffn2_down, final_norm):
    bsz, seq, _ = x.shape
    depth = w_in.shape[0]
    n = bsz * seq
    assert seq % DIL_CHUNK == 0 and seq % (2 * TOKEN_TILE) == 0

    bf = lambda w: w.astype(BF16)
    w_main = bf(w_in[:, :, _rope_column_order()])
    w_vt = bf(jnp.swapaxes(w_in[:, :, 1024:1536], 1, 2))
    g1, u1, d1 = bf(ffn1_gate), bf(ffn1_up), bf(ffn1_down)
    g2, u2, d2 = bf(ffn2_gate), bf(ffn2_up), bf(ffn2_down)
    wo = bf(w_out)
    n1 = ffn1_norm[:, None, :]
    nm = mix_norm[:, None, :]
    n2 = ffn2_norm[:, None, :]
    dg = dil_gain[:, None, :]

    cos, sin = _rope_tables(_to_residue_major(positions, 0))
    h = _to_residue_major(x, 1).reshape(n, D_MODEL)
    for l in range(depth):
        lam_init = 0.8 - 0.6 * math.exp(-0.3 * l)
        h = _ffn_call(h, l, n1, g1, u1, d1)
        qk, dil, vt = _proj_call(h, l, nm, w_main, w_vt, cos, sin, seq)
        vt4 = vt.reshape(bsz, seq // TOKEN_TILE, N_DIFF_HEADS * VT_ROWS, TOKEN_TILE)
        lam_vecs = jnp.stack([lambda_q1[l], lambda_k1[l], lambda_q2[l], lambda_k2[l]])
        d_out = _diff_call(qk.reshape(bsz, seq, QK_WIDTH), vt4, lam_vecs, subln_gain[l][:, None], lam_init)
        a_mix = _dil_call(dil.reshape(bsz, seq, DIL_QKV_WIDTH))
        final = final_norm[None, :] if l == depth - 1 else None
        h = _ffn_call(h, l, n2, g2, u2, d2,
                      mix=(d_out.reshape(n, DIFF_WIDTH), a_mix.reshape(n, DIL_WIDTH), dg, wo),
                      final=final)
    return _from_residue_major(h.reshape(bsz, seq, D_MODEL), 1)
```

```python
import functools
import math

import numpy as np
import jax
import jax.numpy as jnp
from jax import lax
from jax.experimental import pallas as pl
from jax.experimental.pallas import tpu as pltpu

D_MODEL = 1024
HEAD_DIM = 64
N_DIFF_HEADS = 4
DIFF_WIDTH = N_DIFF_HEADS * 2 * HEAD_DIM
DIL_WIDTH = 8 * HEAD_DIM
DIL_BLOCK = 128
DIL_MAX = 16
D_FF = 2816
ROPE_THETA = 10000.0
EPS = 1e-6
SUBLN_EPS = 1e-5
NEG_INF = -1e30
Q_SCALE = HEAD_DIM ** -0.5 * math.log2(math.e)
LAZY_LIMIT = 2.0 ** 90

LANES = 128
SUBLANES = 8
MXU_WIDTH = 256
VMEM_LIMIT = 56 * 1024 * 1024

QK_WIDTH = 2 * DIFF_WIDTH
DIL_QKV_WIDTH = 3 * DIL_WIDTH
PROJ_WIDTH = QK_WIDTH + DIL_QKV_WIDTH
ROPE_WIDTH = 4 * 512

FF_CHUNK = 256
TOKEN_TILE = 512
FFN_TILE = 1024
ACC_ROWS = LANES + SUBLANES
DIL_CHUNK = DIL_BLOCK * DIL_MAX

BF16 = jnp.bfloat16
F32 = jnp.float32
NT_DIMS = (((1,), (1,)), ((), ()))


def _rmsnorm(xf, g, eps):
    return xf * lax.rsqrt(jnp.mean(xf * xf, axis=-1, keepdims=True) + eps) * g


def _params(semantics):
    return pltpu.CompilerParams(dimension_semantics=semantics, vmem_limit_bytes=VMEM_LIMIT)


def _resident(block_shape, index_map):
    return pl.BlockSpec(block_shape, index_map, pipeline_mode=pl.Buffered(1))


def _true_pos(idx):
    return (idx // DIL_BLOCK) * DIL_BLOCK + DIL_MAX * (idx % SUBLANES) + (idx % DIL_BLOCK) // SUBLANES


def _to_residue_major(a, axis):
    shape = a.shape
    a = a.reshape(shape[:axis] + (shape[axis] // DIL_BLOCK, SUBLANES, DIL_MAX) + shape[axis + 1:])
    return jnp.swapaxes(a, axis + 1, axis + 2).reshape(shape)


def _from_residue_major(a, axis):
    shape = a.shape
    a = a.reshape(shape[:axis] + (shape[axis] // DIL_BLOCK, DIL_MAX, SUBLANES) + shape[axis + 1:])
    return jnp.swapaxes(a, axis + 1, axis + 2).reshape(shape)


def _rope_kernel(pos_ref, inv_ref, sign_ref, cos_ref, sin_ref):
    ang = pos_ref[...].astype(F32) * inv_ref[...]
    cos_ref[...] = jnp.cos(ang)
    sin_ref[...] = jnp.sin(ang) * sign_ref[...]


def _rope_tables(positions):
    seq = positions.shape[0]
    inv = 1.0 / (ROPE_THETA ** (jnp.arange(0, HEAD_DIM, 2, dtype=F32) / HEAD_DIM))
    inv_t = jnp.tile(inv, 4)[None, :]
    sign = jnp.concatenate([-jnp.ones((64,), F32), jnp.ones((64,), F32)])[None, :]
    rows = min(seq, 1024)
    return pl.pallas_call(
        _rope_kernel,
        out_shape=(jax.ShapeDtypeStruct((seq, LANES), F32),) * 2,
        grid=(seq // rows,),
        in_specs=[pl.BlockSpec((rows, 1), lambda i: (i, 0)),
                  pl.BlockSpec((1, LANES), lambda i: (0, 0)),
                  pl.BlockSpec((1, LANES), lambda i: (0, 0))],
        out_specs=(pl.BlockSpec((rows, LANES), lambda i: (i, 0)),) * 2,
        compiler_params=_params(("arbitrary",)),
        name="rope_tables",
    )(positions.reshape(seq, 1), inv_t, sign)


def _swiglu_residual(xf, g, wg_ref, wu_ref, wd_ref, h_ref):
    xn = _rmsnorm(xf, g, EPS).astype(BF16)
    for c in range(D_FF // FF_CHUNK):
        sl = slice(c * FF_CHUNK, (c + 1) * FF_CHUNK)
        a = jnp.dot(xn, wg_ref[:, sl], preferred_element_type=F32)
        u = jnp.dot(xn, wu_ref[:, sl], preferred_element_type=F32)
        h_ref[:, sl] = (a * jax.nn.sigmoid(a) * u).astype(BF16)
    y = jnp.dot(h_ref[...], wd_ref[...], preferred_element_type=F32)
    return xf + 0.5 * y


def _ffn_kernel(*refs, has_mix, has_final, tokens_in, tokens_out):
    refs = list(refs)
    x_ref = refs.pop(0)
    if has_mix:
        d_ref, a_ref, dg_ref, wo_ref = (refs.pop(0) for _ in range(4))
    g_ref, wg_ref, wu_ref, wd_ref = (refs.pop(0) for _ in range(4))
    if has_final:
        fin_ref = refs.pop(0)
    o_ref, h_ref = refs
    xf = x_ref[...]
    if tokens_in:
        xf = jnp.swapaxes(xf, 1, 2).reshape(-1, D_MODEL)
    if has_mix:
        a_out = _rmsnorm(a_ref[...], dg_ref[...], EPS).astype(BF16)
        xf = xf + jnp.dot(d_ref[...], wo_ref[:DIFF_WIDTH, :], preferred_element_type=F32)
        xf = xf + jnp.dot(a_out, wo_ref[DIFF_WIDTH:, :], preferred_element_type=F32)
    out = _swiglu_residual(xf, g_ref[...], wg_ref, wu_ref, wd_ref, h_ref)
    if has_final:
        out = _rmsnorm(out, fin_ref[...], EPS)
    if tokens_out:
        out = jnp.swapaxes(out.reshape(-1, DIL_MAX, SUBLANES, D_MODEL), 1, 2)
    o_ref[...] = out


def _ffn_call(x2, layer, norm, wg, wu, wd, mix=None, final=None, tokens_in=False, tokens_out=False):
    n = x2.size // D_MODEL
    tm = FFN_TILE
    row = lambda i: (i, 0)
    const2 = lambda i: (0, 0)
    lay3 = lambda i: (layer, 0, 0)
    tok_block = (tm // DIL_BLOCK, SUBLANES, DIL_MAX, D_MODEL)
    tok_spec = pl.BlockSpec(tok_block, lambda i: (i, 0, 0, 0))
    args = [x2]
    specs = [tok_spec if tokens_in else pl.BlockSpec((tm, D_MODEL), row)]
    if mix is not None:
        d_out, a_mix, dil_gain, w_out = mix
        args += [d_out, a_mix, dil_gain, w_out]
        specs += [pl.BlockSpec((tm, DIFF_WIDTH), row), pl.BlockSpec((tm, DIL_WIDTH), row),
                  pl.BlockSpec((None, 1, DIL_WIDTH), lay3),
                  _resident((None, D_MODEL, D_MODEL), lay3)]
    args += [norm, wg, wu, wd]
    specs += [pl.BlockSpec((None, 1, D_MODEL), lay3),
              _resident((None, D_MODEL, D_FF), lay3),
              _resident((None, D_MODEL, D_FF), lay3),
              _resident((None, D_FF, D_MODEL), lay3)]
    if final is not None:
        args.append(final)
        specs.append(pl.BlockSpec((1, D_MODEL), const2))
    return pl.pallas_call(
        functools.partial(_ffn_kernel, has_mix=mix is not None, has_final=final is not None,
                          tokens_in=tokens_in, tokens_out=tokens_out),
        out_shape=jax.ShapeDtypeStruct((n // DIL_BLOCK, SUBLANES, DIL_MAX, D_MODEL) if tokens_out else (n, D_MODEL), F32),
        grid=(n // tm,),
        in_specs=specs,
        out_specs=tok_spec if tokens_out else pl.BlockSpec((tm, D_MODEL), row),
        scratch_shapes=[pltpu.VMEM((tm, D_FF), BF16)],
        compiler_params=_params(("parallel",)),
        name="mix_ffn" if mix is not None else "ffn",
    )(*args)


def _proj_kernel(x_ref, g_ref, w_ref, wvt_ref, cos_ref, sin_ref, qk_ref, dil_ref, vt_ref):
    xn = _rmsnorm(x_ref[...], g_ref[...], EPS).astype(BF16)
    cos = cos_ref[...]
    sin = sin_ref[...]
    for c in range(PROJ_WIDTH // MXU_WIDTH):
        lo = c * MXU_WIDTH
        t = jnp.dot(xn, w_ref[:, lo:lo + MXU_WIDTH], preferred_element_type=F32)
        if lo < ROPE_WIDTH:
            scale = Q_SCALE if lo % 1024 < 512 else 1.0
            halves = []
            for half in range(MXU_WIDTH // LANES):
                th = t[:, half * LANES:(half + 1) * LANES]
                halves.append((th * cos + pltpu.roll(th, 64, 1) * sin) * scale)
            t = jnp.concatenate(halves, axis=1)
        if lo < QK_WIDTH:
            qk_ref[:, lo:lo + MXU_WIDTH] = t.astype(BF16)
        else:
            dil_ref[:, lo - QK_WIDTH:lo - QK_WIDTH + MXU_WIDTH] = t
    vt = lax.dot_general(wvt_ref[...], xn, NT_DIMS, preferred_element_type=F32)
    vt_ref[...] = vt.astype(BF16)


def _proj_call(x2, layer, norm, w_main, w_vt, cos, sin, seq):
    n = x2.shape[0]
    tm = TOKEN_TILE
    tiles_per_seq = seq // tm
    row = lambda i: (i, 0)
    lay3 = lambda i: (layer, 0, 0)
    pos = lambda i: (i % tiles_per_seq, 0)
    return pl.pallas_call(
        _proj_kernel,
        out_shape=(jax.ShapeDtypeStruct((n, QK_WIDTH), BF16),
                   jax.ShapeDtypeStruct((n, DIL_QKV_WIDTH), F32),
                   jax.ShapeDtypeStruct((n // tm, DIFF_WIDTH, tm), BF16)),
        grid=(n // tm,),
        in_specs=[pl.BlockSpec((tm, D_MODEL), row),
                  pl.BlockSpec((None, 1, D_MODEL), lay3),
                  _resident((None, D_MODEL, PROJ_WIDTH), lay3),
                  _resident((None, DIFF_WIDTH, D_MODEL), lay3),
                  pl.BlockSpec((tm, LANES), pos),
                  pl.BlockSpec((tm, LANES), pos)],
        out_specs=(pl.BlockSpec((tm, QK_WIDTH), row),
                   pl.BlockSpec((tm, DIL_QKV_WIDTH), row),
                   pl.BlockSpec((None, DIFF_WIDTH, tm), lambda i: (i, 0, 0))),
        compiler_params=_params(("parallel",)),
        name="proj_rope",
    )(x2, norm, w_main, w_vt, cos, sin)


def _diff_kernel(lam_ref, gain_ref, q_ref, k_ref, vt_ref, o_ref, qall_ref, acc_ref, *, lam_init, blk):
    i = pl.program_id(2)
    ncol = 4 * blk
    lane = lax.broadcasted_iota(jnp.int32, (blk, LANES), 1)
    is_first = (lane % 64) < 32
    for hh in range(2):
        q = q_ref[0, hh * blk:(hh + 1) * blk, :]
        zero = jnp.zeros_like(q)
        qall_ref[(2 * hh) * blk:(2 * hh + 1) * blk, :] = jnp.where(is_first, q, zero)
        qall_ref[(2 * hh + 1) * blk:(2 * hh + 2) * blk, :] = jnp.where(is_first, zero, q)

    def scores(j, nkb, col0, masked_cols):
        keys = nkb * blk
        kb = k_ref[0, pl.ds(pl.multiple_of(j * blk, blk), keys), :]
        s = lax.dot_general(kb, qall_ref[col0:, :], NT_DIMS, preferred_element_type=F32)
        if masked_cols:
            key_idx = lax.broadcasted_iota(jnp.int32, s.shape, 0)
            col_idx = lax.broadcasted_iota(jnp.int32, s.shape, 1)
            visible = (_true_pos(key_idx) <= _true_pos(col_idx % blk)) | (col_idx >= masked_cols)
            s = jnp.where(visible, s, NEG_INF)
        return s

    def values(j, nkb, p):
        pb = p.astype(BF16)
        num = sum(jnp.dot(vt_ref[0, j + t], pb[t * blk:(t + 1) * blk], preferred_element_type=F32)
                  for t in range(nkb))
        return num, jnp.sum(p, axis=0, keepdims=True)

    def exact_block(j, nkb, m_old, col0, masked_cols):
        s = scores(j, nkb, col0, masked_cols)
        m_prev = m_old[:, col0:]
        m_new = jnp.maximum(m_prev, jnp.max(s, axis=0, keepdims=True))
        alpha = jnp.exp2(m_prev - m_new)
        num, den = values(j, nkb, jnp.exp2(s - m_new))
        acc_ref[:LANES, col0:] = alpha * acc_ref[:LANES, col0:] + num
        acc_ref[LANES:LANES + 1, col0:] = alpha * acc_ref[LANES:LANES + 1, col0:] + den
        if col0:
            m_new = jnp.concatenate([m_old[:, :col0], m_new], axis=1)
        return m_new

    def single_pass(j, nkb, shift, col0, masked_cols, first=False):
        s = scores(j, nkb, col0, masked_cols)
        if first:
            shift = s[0:1, :]
        num, den = values(j, nkb, jnp.exp2(s - shift[:, col0:]))
        if first:
            acc_ref[:LANES, :] = num
            acc_ref[LANES:LANES + 1, :] = den
        else:
            acc_ref[:LANES, col0:] += num
            acc_ref[LANES:LANES + 1, col0:] += den
        return shift

    def attend(exact):
        if exact:
            acc_ref[...] = jnp.zeros_like(acc_ref)
            m = jnp.full((1, ncol), NEG_INF, F32)
            m = exact_block(2 * i, 1, m, 0, 2 * blk)
            m = exact_block(2 * i + 1, 1, m, 2 * blk, 2 * blk)
            lax.fori_loop(0, i, lambda jj, mm: exact_block(2 * jj, 2, mm, 0, 0), m)
            return
        shift = single_pass(2 * i, 1, None, 0, 2 * blk, first=True)
        single_pass(2 * i + 1, 1, shift, 2 * blk, 2 * blk)

        def two_pairs(jj, carry):
            single_pass(4 * jj, 2, shift, 0, 0)
            single_pass(4 * jj + 2, 2, shift, 0, 0)
            return carry

        lax.fori_loop(0, i // 2, two_pairs, 0)

        @pl.when(i % 2 == 1)
        def _():
            single_pass(2 * (i - 1), 2, shift, 0, 0)

    lv = lam_ref[...]
    lam = (jnp.exp(jnp.sum(lv[0:1] * lv[1:2], axis=1, keepdims=True))
           - jnp.exp(jnp.sum(lv[2:3] * lv[3:4], axis=1, keepdims=True)) + lam_init)

    def finalize():
        mean_sq = []
        for hh in range(2):
            c0 = slice((2 * hh) * blk, (2 * hh + 1) * blk)
            c1 = slice((2 * hh + 1) * blk, (2 * hh + 2) * blk)
            inv0 = 1.0 / acc_ref[LANES:LANES + 1, c0]
            inv1 = lam / acc_ref[LANES:LANES + 1, c1]
            o = acc_ref[:LANES, c0] * inv0 - acc_ref[:LANES, c1] * inv1
            mean_sq.append(jnp.mean(o * o, axis=0, keepdims=True))
            y = o * lax.rsqrt(mean_sq[-1] + SUBLN_EPS) * gain_ref[...] * (1.0 - lam_init)
            o_ref[0, hh * blk:(hh + 1) * blk, :] = y.T.astype(BF16)
        return mean_sq

    attend(exact=False)
    mean_sq = finalize()
    finite = jnp.sum((mean_sq[0] + mean_sq[1]) * 0.0) == 0.0
    stale = jnp.logical_not((jnp.max(acc_ref[LANES:LANES + 1, :]) <= LAZY_LIMIT) & finite)

    @pl.when(stale)
    def _():
        attend(exact=True)
        finalize()


def _diff_call(qk3, vt4, lam_vecs, gain_col, lam_init):
    bsz, seq, _ = qk3.shape
    blk = TOKEN_TILE
    nblk = seq // blk
    k_lane_block = DIFF_WIDTH // LANES
    return pl.pallas_call(
        functools.partial(_diff_kernel, lam_init=lam_init, blk=blk),
        out_shape=jax.ShapeDtypeStruct((bsz, seq, DIFF_WIDTH), BF16),
        grid=(bsz, N_DIFF_HEADS, nblk // 2),
        in_specs=[pl.BlockSpec((4, HEAD_DIM), lambda b, h, i: (0, 0)),
                  pl.BlockSpec((LANES, 1), lambda b, h, i: (0, 0)),
                  pl.BlockSpec((1, 2 * blk, LANES), lambda b, h, i: (b, i, h)),
                  pl.BlockSpec((1, seq, LANES), lambda b, h, i: (b, 0, k_lane_block + h)),
                  pl.BlockSpec((1, nblk, LANES, blk), lambda b, h, i: (b, 0, h, 0))],
        out_specs=pl.BlockSpec((1, 2 * blk, LANES), lambda b, h, i: (b, i, h)),
        scratch_shapes=[pltpu.VMEM((4 * blk, LANES), BF16), pltpu.VMEM((ACC_ROWS, 4 * blk), F32)],
        compiler_params=_params(("parallel", "parallel", "arbitrary")),
        name="diff_attn",
    )(lam_vecs, gain_col, qk3, qk3, vt4)


def _dil_kernel(q_ref, k_ref, v_ref, o_ref, m_ref, l_ref, acc_ref):
    c = pl.program_id(2)
    base = c * DIL_CHUNK
    blk = DIL_BLOCK
    lane = lax.broadcasted_iota(jnp.int32, (blk, LANES), 1)
    v_first = lane < 64
    lane2 = lax.broadcasted_iota(jnp.int32, (2 * blk, LANES), 1)
    row2 = lax.broadcasted_iota(jnp.int32, (2 * blk, LANES), 0)
    q_keep = ((lane2 % 64) < 32) == (row2 < blk)
    qrow = lax.broadcasted_iota(jnp.int32, (2 * blk, 2 * blk), 0) % blk
    kcol = lax.broadcasted_iota(jnp.int32, (2 * blk, 2 * blk), 1)

    def band(qpos, kpos):
        dist = qpos + blk - kpos
        return (dist >= 0) & (dist <= blk)

    def score_cap(visible):
        return jnp.where(visible, jnp.inf, NEG_INF).astype(F32)

    pos4 = lambda idx: 32 * (idx // 32) + 4 * (idx % SUBLANES) + (idx % 32) // SUBLANES
    band1 = band(_true_pos(qrow), _true_pos(kcol))
    band4 = band(pos4(qrow), blk * (kcol // blk) + pos4(kcol % blk))
    band16 = band(qrow, kcol)

    def strip(start):
        return pl.ds(start if isinstance(start, int) else pl.multiple_of(start, SUBLANES), SUBLANES)

    def strips(ref, starts):
        return jnp.concatenate([ref[0, strip(s), :] for s in starts], axis=0)

    def attend(q, k, v, cap):
        qb = q.astype(BF16)
        q2 = jnp.concatenate([qb, qb], axis=0)
        q2 = jnp.where(q_keep, q2, jnp.zeros_like(q2))
        s = lax.dot_general(q2, k, NT_DIMS, preferred_element_type=F32)
        s = jnp.minimum(s, cap)
        m = jnp.max(s, axis=1, keepdims=True)
        p = jnp.exp2(s - m).astype(BF16)
        v_ones = jnp.concatenate([v, jnp.ones_like(v)], axis=1)
        num = jnp.dot(p, v_ones, preferred_element_type=F32)
        m_pair = jnp.where(v_first, jnp.broadcast_to(m[:blk], (blk, LANES)), jnp.broadcast_to(m[blk:], (blk, LANES)))
        return (m_pair, jnp.where(v_first, num[:blk, LANES:], num[blk:, LANES:]),
                jnp.where(v_first, num[:blk, :LANES], num[blk:, :LANES]))

    def merge(q_starts, m, den, num):
        for t, s0 in enumerate(q_starts):
            rows = strip(s0)
            sl = slice(t * SUBLANES, (t + 1) * SUBLANES)
            m_st = m_ref[rows, :]
            m_new = jnp.maximum(m_st, m[sl])
            w_st = jnp.exp2(m_st - m_new)
            w_new = jnp.exp2(m[sl] - m_new)
            m_ref[rows, :] = m_new
            l_ref[rows, :] = w_st * l_ref[rows, :] + w_new * den[sl]
            acc_ref[rows, :] = w_st * acc_ref[rows, :] + w_new * num[sl]

    first_key = jnp.where(c > 0, 0, blk)

    cap_first = lambda in_band: score_cap(in_band & (kcol >= first_key))
    both = lambda prev, own: jnp.concatenate([prev, own], axis=0)
    gather = lambda ref, starts: strips(ref, starts).astype(BF16)

    cap_a, cap_b = cap_first(band1), score_cap(band1)
    start0 = jnp.maximum(base - blk, 0)
    load = lambda ref, start: ref[0, pl.ds(pl.multiple_of(start, blk), blk), :].astype(BF16)
    k_prev, v_prev = load(k_ref, start0), load(v_ref, start0)
    for u in range(DIL_MAX):
        rows = slice(u * blk, (u + 1) * blk)
        k_own, v_own = load(k_ref, base + u * blk), load(v_ref, base + u * blk)
        m, den, num = attend(q_ref[0, rows, :], both(k_prev, k_own), both(v_prev, v_own), cap_a if u == 0 else cap_b)
        m_ref[rows, :] = m
        l_ref[rows, :] = den
        acc_ref[rows, :] = num
        k_prev, v_prev = k_own, v_own

    cap_a, cap_b = cap_first(band4), score_cap(band4)
    start0 = jnp.maximum(base - 512, 0)
    for r4 in range(4):
        block_rows = lambda bq: [blk * (4 * bq + ul) + SUBLANES * (4 * a + r4) for ul in range(4) for a in range(4)]
        prev = [start0 + s for s in block_rows(0)]
        k_prev, v_prev = gather(k_ref, prev), gather(v_ref, prev)
        for bq in range(4):
            rel = block_rows(bq)
            own = [base + s for s in rel]
            k_own, v_own = gather(k_ref, own), gather(v_ref, own)
            m, den, num = attend(strips(q_ref, rel), both(k_prev, k_own), both(v_prev, v_own),
                                 cap_a if bq == 0 else cap_b)
            merge(rel, m, den, num)
            k_prev, v_prev = k_own, v_own

    cap_a = cap_first(band16)
    start0 = jnp.maximum(base - DIL_CHUNK, 0)
    for r in range(DIL_MAX):
        rel = [blk * u + SUBLANES * r for u in range(DIL_MAX)]
        k = both(gather(k_ref, [start0 + s for s in rel]), gather(k_ref, [base + s for s in rel]))
        v = both(gather(v_ref, [start0 + s for s in rel]), gather(v_ref, [base + s for s in rel]))
        m, den, num = attend(strips(q_ref, rel), k, v, cap_a)
        merge(rel, m, den, num)

    o_ref[0] = acc_ref[...] / l_ref[...]


def _dil_call(dil3):
    bsz, seq, _ = dil3.shape
    pairs = DIL_WIDTH // LANES
    return pl.pallas_call(
        _dil_kernel,
        out_shape=jax.ShapeDtypeStruct((bsz, seq, DIL_WIDTH), F32),
        grid=(bsz, pairs, seq // DIL_CHUNK),
        in_specs=[pl.BlockSpec((1, DIL_CHUNK, LANES), lambda b, g, c: (b, c, g)),
                  pl.BlockSpec((1, seq, LANES), lambda b, g, c: (b, 0, pairs + g)),
                  pl.BlockSpec((1, seq, LANES), lambda b, g, c: (b, 0, 2 * pairs + g))],
        out_specs=pl.BlockSpec((1, DIL_CHUNK, LANES), lambda b, g, c: (b, c, g)),
        scratch_shapes=[pltpu.VMEM((DIL_CHUNK, LANES), F32)] * 3,
        compiler_params=_params(("parallel", "parallel", "arbitrary")),
        name="dilated",
    )(dil3, dil3, dil3)


def _rope_column_order():
    half = HEAD_DIM // 2
    cols = []
    for start in (0, 512, 1536, 2048):
        for g in range(512 // LANES):
            a = start + g * LANES
            b = a + HEAD_DIM
            cols += list(range(a, a + half)) + list(range(b, b + half))
            cols += list(range(a + half, a + HEAD_DIM)) + list(range(b + half, b + HEAD_DIM))
    cols += list(range(2560, 3072))
    return np.asarray(cols, dtype=np.int32)


def kernel(x, positions, ffn1_norm, ffn1_gate, ffn1_up, ffn1_down, mix_norm, w_in, lambda_q1, lambda_k1, lambda_q2, lambda_k2, subln_gain, dil_gain, w_out, ffn2_norm, ffn2_gate, ffn2_up, ffn2_down, final_norm):
    bsz, seq, _ = x.shape
    depth = w_in.shape[0]
    n = bsz * seq
    assert seq % DIL_CHUNK == 0 and seq % (2 * TOKEN_TILE) == 0

    bf = lambda w: w.astype(BF16)
    w_main = bf(w_in[:, :, _rope_column_order()])
    w_vt = bf(jnp.swapaxes(w_in[:, :, 1024:1536], 1, 2))
    g1, u1, d1 = bf(ffn1_gate), bf(ffn1_up), bf(ffn1_down)
    g2, u2, d2 = bf(ffn2_gate), bf(ffn2_up), bf(ffn2_down)
    wo = bf(w_out)
    n1 = ffn1_norm[:, None, :]
    nm = mix_norm[:, None, :]
    n2 = ffn2_norm[:, None, :]
    dg = dil_gain[:, None, :]

    cos, sin = _rope_tables(_to_residue_major(positions, 0))
    h = x.reshape(n // DIL_BLOCK, SUBLANES, DIL_MAX, D_MODEL)
    for l in range(depth):
        lam_init = 0.8 - 0.6 * math.exp(-0.3 * l)
        h = _ffn_call(h, l, n1, g1, u1, d1, tokens_in=l == 0)
        qk, dil, vt = _proj_call(h, l, nm, w_main, w_vt, cos, sin, seq)
        vt4 = vt.reshape(bsz, seq // TOKEN_TILE, DIFF_WIDTH, TOKEN_TILE)
        lam_vecs = jnp.stack([lambda_q1[l], lambda_k1[l], lambda_q2[l], lambda_k2[l]])
        d_out = _diff_call(qk.reshape(bsz, seq, QK_WIDTH), vt4, lam_vecs, subln_gain[l][:, None], lam_init)
        a_mix = _dil_call(dil.reshape(bsz, seq, DIL_QKV_WIDTH))
        final = final_norm[None, :] if l == depth - 1 else None
        h = _ffn_call(h, l, n2, g2, u2, d2,
                      mix=(d_out.reshape(n, DIFF_WIDTH), a_mix.reshape(n, DIL_WIDTH), dg, wo),
                      final=final, tokens_out=l == depth - 1)
    return h.reshape(bsz, seq, D_MODEL)
```

```python
import functools
import math

import numpy as np
import jax
import jax.numpy as jnp
from jax import lax
from jax.experimental import pallas as pl
from jax.experimental.pallas import tpu as pltpu

D_MODEL = 1024
HEAD_DIM = 64
N_DIFF_HEADS = 4
DIFF_WIDTH = N_DIFF_HEADS * 2 * HEAD_DIM
DIL_WIDTH = 8 * HEAD_DIM
DIL_BLOCK = 128
DIL_MAX = 16
D_FF = 2816
ROPE_THETA = 10000.0
EPS = 1e-6
SUBLN_EPS = 1e-5
NEG_INF = -1e30
Q_SCALE = HEAD_DIM ** -0.5 * math.log2(math.e)
LAZY_LIMIT = 2.0 ** 90

LANES = 128
SUBLANES = 8
MXU_WIDTH = 256
VMEM_LIMIT = 56 * 1024 * 1024

QK_WIDTH = 2 * DIFF_WIDTH
DIL_QKV_WIDTH = 3 * DIL_WIDTH
PROJ_WIDTH = QK_WIDTH + DIL_QKV_WIDTH
ROPE_WIDTH = 4 * 512

FF_CHUNK = 256
TOKEN_TILE = 512
PROJ_TILE = 1024
FFN_TILE = 1024
ACC_ROWS = LANES + SUBLANES
DIL_CHUNK = DIL_BLOCK * DIL_MAX

BF16 = jnp.bfloat16
F32 = jnp.float32
NT_DIMS = (((1,), (1,)), ((), ()))


def _rmsnorm(xf, g, eps):
    return xf * lax.rsqrt(jnp.mean(xf * xf, axis=-1, keepdims=True) + eps) * g


def _params(semantics):
    return pltpu.CompilerParams(dimension_semantics=semantics, vmem_limit_bytes=VMEM_LIMIT)


def _resident(block_shape, index_map):
    return pl.BlockSpec(block_shape, index_map, pipeline_mode=pl.Buffered(1))


def _true_pos(idx):
    return (idx // DIL_BLOCK) * DIL_BLOCK + DIL_MAX * (idx % SUBLANES) + (idx % DIL_BLOCK) // SUBLANES


def _to_residue_major(a, axis):
    shape = a.shape
    a = a.reshape(shape[:axis] + (shape[axis] // DIL_BLOCK, SUBLANES, DIL_MAX) + shape[axis + 1:])
    return jnp.swapaxes(a, axis + 1, axis + 2).reshape(shape)


def _from_residue_major(a, axis):
    shape = a.shape
    a = a.reshape(shape[:axis] + (shape[axis] // DIL_BLOCK, DIL_MAX, SUBLANES) + shape[axis + 1:])
    return jnp.swapaxes(a, axis + 1, axis + 2).reshape(shape)


def _rope_kernel(pos_ref, inv_ref, sign_ref, cos_ref, sin_ref):
    ang = pos_ref[...].astype(F32) * inv_ref[...]
    cos_ref[...] = jnp.cos(ang)
    sin_ref[...] = jnp.sin(ang) * sign_ref[...]


def _rope_tables(positions):
    seq = positions.shape[0]
    inv = 1.0 / (ROPE_THETA ** (jnp.arange(0, HEAD_DIM, 2, dtype=F32) / HEAD_DIM))
    inv_t = jnp.tile(inv, 4)[None, :]
    sign = jnp.concatenate([-jnp.ones((64,), F32), jnp.ones((64,), F32)])[None, :]
    rows = min(seq, 1024)
    return pl.pallas_call(
        _rope_kernel,
        out_shape=(jax.ShapeDtypeStruct((seq, LANES), F32),) * 2,
        grid=(seq // rows,),
        in_specs=[pl.BlockSpec((rows, 1), lambda i: (i, 0)),
                  pl.BlockSpec((1, LANES), lambda i: (0, 0)),
                  pl.BlockSpec((1, LANES), lambda i: (0, 0))],
        out_specs=(pl.BlockSpec((rows, LANES), lambda i: (i, 0)),) * 2,
        compiler_params=_params(("arbitrary",)),
        name="rope_tables",
    )(positions.reshape(seq, 1), inv_t, sign)


def _swiglu_residual(xf, g, wg_ref, wu_ref, wd_ref, h_ref):
    xn = _rmsnorm(xf, g, EPS).astype(BF16)
    for c in range(D_FF // FF_CHUNK):
        sl = slice(c * FF_CHUNK, (c + 1) * FF_CHUNK)
        a = jnp.dot(xn, wg_ref[:, sl], preferred_element_type=F32)
        u = jnp.dot(xn, wu_ref[:, sl], preferred_element_type=F32)
        h_ref[:, sl] = (a * jax.nn.sigmoid(a) * u).astype(BF16)
    y = jnp.dot(h_ref[...], wd_ref[...], preferred_element_type=F32)
    return xf + 0.5 * y


def _ffn_kernel(*refs, has_mix, has_final, tokens_in, tokens_out):
    refs = list(refs)
    x_ref = refs.pop(0)
    if has_mix:
        d_ref, a_ref, dg_ref, wo_ref = (refs.pop(0) for _ in range(4))
    g_ref, wg_ref, wu_ref, wd_ref = (refs.pop(0) for _ in range(4))
    if has_final:
        fin_ref = refs.pop(0)
    o_ref, h_ref = refs
    xf = x_ref[...]
    if tokens_in:
        xf = jnp.swapaxes(xf, 1, 2).reshape(-1, D_MODEL)
    if has_mix:
        a_out = _rmsnorm(a_ref[...], dg_ref[...], EPS).astype(BF16)
        xf = xf + jnp.dot(d_ref[...], wo_ref[:DIFF_WIDTH, :], preferred_element_type=F32)
        xf = xf + jnp.dot(a_out, wo_ref[DIFF_WIDTH:, :], preferred_element_type=F32)
    out = _swiglu_residual(xf, g_ref[...], wg_ref, wu_ref, wd_ref, h_ref)
    if has_final:
        out = _rmsnorm(out, fin_ref[...], EPS)
    if tokens_out:
        out = jnp.swapaxes(out.reshape(-1, DIL_MAX, SUBLANES, D_MODEL), 1, 2)
    o_ref[...] = out


def _ffn_call(x2, layer, norm, wg, wu, wd, mix=None, final=None, tokens_in=False, tokens_out=False):
    n = x2.size // D_MODEL
    tm = FFN_TILE
    row = lambda i: (i, 0)
    const2 = lambda i: (0, 0)
    lay3 = lambda i: (layer, 0, 0)
    tok_block = (tm // DIL_BLOCK, SUBLANES, DIL_MAX, D_MODEL)
    tok_spec = pl.BlockSpec(tok_block, lambda i: (i, 0, 0, 0))
    args = [x2]
    specs = [tok_spec if tokens_in else pl.BlockSpec((tm, D_MODEL), row)]
    if mix is not None:
        d_out, a_mix, dil_gain, w_out = mix
        args += [d_out, a_mix, dil_gain, w_out]
        specs += [pl.BlockSpec((tm, DIFF_WIDTH), row), pl.BlockSpec((tm, DIL_WIDTH), row),
                  pl.BlockSpec((None, 1, DIL_WIDTH), lay3),
                  _resident((None, D_MODEL, D_MODEL), lay3)]
    args += [norm, wg, wu, wd]
    specs += [pl.BlockSpec((None, 1, D_MODEL), lay3),
              _resident((None, D_MODEL, D_FF), lay3),
              _resident((None, D_MODEL, D_FF), lay3),
              _resident((None, D_FF, D_MODEL), lay3)]
    if final is not None:
        args.append(final)
        specs.append(pl.BlockSpec((1, D_MODEL), const2))
    return pl.pallas_call(
        functools.partial(_ffn_kernel, has_mix=mix is not None, has_final=final is not None,
                          tokens_in=tokens_in, tokens_out=tokens_out),
        out_shape=jax.ShapeDtypeStruct((n // DIL_BLOCK, SUBLANES, DIL_MAX, D_MODEL) if tokens_out else (n, D_MODEL), F32),
        grid=(n // tm,),
        in_specs=specs,
        out_specs=tok_spec if tokens_out else pl.BlockSpec((tm, D_MODEL), row),
        scratch_shapes=[pltpu.VMEM((tm, D_FF), BF16)],
        compiler_params=_params(("parallel",)),
        name="mix_ffn" if mix is not None else "ffn",
    )(*args)


def _proj_kernel(x_ref, g_ref, w_ref, wvt_ref, cos_ref, sin_ref, qk_ref, dil_ref, vt_ref):
    xn = _rmsnorm(x_ref[...], g_ref[...], EPS).astype(BF16)
    cos = cos_ref[...]
    sin = sin_ref[...]
    for c in range(PROJ_WIDTH // MXU_WIDTH):
        lo = c * MXU_WIDTH
        t = jnp.dot(xn, w_ref[:, lo:lo + MXU_WIDTH], preferred_element_type=F32)
        if lo < ROPE_WIDTH:
            scale = Q_SCALE if lo % 1024 < 512 else 1.0
            halves = []
            for half in range(MXU_WIDTH // LANES):
                th = t[:, half * LANES:(half + 1) * LANES]
                halves.append((th * cos + pltpu.roll(th, 64, 1) * sin) * scale)
            t = jnp.concatenate(halves, axis=1)
        if lo < QK_WIDTH:
            qk_ref[:, lo:lo + MXU_WIDTH] = t.astype(BF16)
        else:
            dil_ref[:, lo - QK_WIDTH:lo - QK_WIDTH + MXU_WIDTH] = t
    vt = lax.dot_general(wvt_ref[...], xn, NT_DIMS, preferred_element_type=F32)
    for t in range(vt_ref.shape[0]):
        vt_ref[t] = vt[:, t * TOKEN_TILE:(t + 1) * TOKEN_TILE].astype(BF16)


def _proj_call(x2, layer, norm, w_main, w_vt, cos, sin, seq):
    n = x2.shape[0]
    tm = PROJ_TILE
    tiles_per_seq = seq // tm
    row = lambda i: (i, 0)
    lay3 = lambda i: (layer, 0, 0)
    pos = lambda i: (i % tiles_per_seq, 0)
    return pl.pallas_call(
        _proj_kernel,
        out_shape=(jax.ShapeDtypeStruct((n, QK_WIDTH), BF16),
                   jax.ShapeDtypeStruct((n, DIL_QKV_WIDTH), F32),
                   jax.ShapeDtypeStruct((n // TOKEN_TILE, DIFF_WIDTH, TOKEN_TILE), BF16)),
        grid=(n // tm,),
        in_specs=[pl.BlockSpec((tm, D_MODEL), row),
                  pl.BlockSpec((None, 1, D_MODEL), lay3),
                  _resident((None, D_MODEL, PROJ_WIDTH), lay3),
                  _resident((None, DIFF_WIDTH, D_MODEL), lay3),
                  pl.BlockSpec((tm, LANES), pos),
                  pl.BlockSpec((tm, LANES), pos)],
        out_specs=(pl.BlockSpec((tm, QK_WIDTH), row),
                   pl.BlockSpec((tm, DIL_QKV_WIDTH), row),
                   pl.BlockSpec((tm // TOKEN_TILE, DIFF_WIDTH, TOKEN_TILE), lambda i: (i, 0, 0))),
        compiler_params=_params(("parallel",)),
        name="proj_rope",
    )(x2, norm, w_main, w_vt, cos, sin)


def _diff_kernel(lam_ref, gain_ref, q_ref, k_ref, vt_ref, o_ref, qall_ref, acc_ref, *, lam_init, blk):
    i = pl.program_id(2)
    ncol = 4 * blk
    lane = lax.broadcasted_iota(jnp.int32, (blk, LANES), 1)
    is_first = (lane % 64) < 32
    for hh in range(2):
        q = q_ref[0, hh * blk:(hh + 1) * blk, :]
        zero = jnp.zeros_like(q)
        qall_ref[(2 * hh) * blk:(2 * hh + 1) * blk, :] = jnp.where(is_first, q, zero)
        qall_ref[(2 * hh + 1) * blk:(2 * hh + 2) * blk, :] = jnp.where(is_first, zero, q)

    def scores(j, nkb, col0, masked_cols):
        keys = nkb * blk
        kb = k_ref[0, pl.ds(pl.multiple_of(j * blk, blk), keys), :]
        s = lax.dot_general(kb, qall_ref[col0:, :], NT_DIMS, preferred_element_type=F32)
        if masked_cols:
            key_idx = lax.broadcasted_iota(jnp.int32, s.shape, 0)
            col_idx = lax.broadcasted_iota(jnp.int32, s.shape, 1)
            visible = (_true_pos(key_idx) <= _true_pos(col_idx % blk)) | (col_idx >= masked_cols)
            s = jnp.where(visible, s, NEG_INF)
        return s

    def values(j, nkb, p):
        pb = p.astype(BF16)
        num = sum(jnp.dot(vt_ref[0, j + t], pb[t * blk:(t + 1) * blk], preferred_element_type=F32)
                  for t in range(nkb))
        return num, jnp.sum(p, axis=0, keepdims=True)

    def exact_block(j, nkb, m_old, col0, masked_cols):
        s = scores(j, nkb, col0, masked_cols)
        m_prev = m_old[:, col0:]
        m_new = jnp.maximum(m_prev, jnp.max(s, axis=0, keepdims=True))
        alpha = jnp.exp2(m_prev - m_new)
        num, den = values(j, nkb, jnp.exp2(s - m_new))
        acc_ref[:LANES, col0:] = alpha * acc_ref[:LANES, col0:] + num
        acc_ref[LANES:LANES + 1, col0:] = alpha * acc_ref[LANES:LANES + 1, col0:] + den
        if col0:
            m_new = jnp.concatenate([m_old[:, :col0], m_new], axis=1)
        return m_new

    def single_pass(j, nkb, shift, col0, masked_cols, first=False):
        s = scores(j, nkb, col0, masked_cols)
        if first:
            shift = s[0:1, :]
        num, den = values(j, nkb, jnp.exp2(s - shift[:, col0:]))
        if first:
            acc_ref[:LANES, :] = num
            acc_ref[LANES:LANES + 1, :] = den
        else:
            acc_ref[:LANES, col0:] += num
            acc_ref[LANES:LANES + 1, col0:] += den
        return shift

    def attend(exact):
        if exact:
            acc_ref[...] = jnp.zeros_like(acc_ref)
            m = jnp.full((1, ncol), NEG_INF, F32)
            m = exact_block(2 * i, 1, m, 0, 2 * blk)
            m = exact_block(2 * i + 1, 1, m, 2 * blk, 2 * blk)
            lax.fori_loop(0, i, lambda jj, mm: exact_block(2 * jj, 2, mm, 0, 0), m)
            return
        shift = single_pass(2 * i, 1, None, 0, 2 * blk, first=True)
        single_pass(2 * i + 1, 1, shift, 2 * blk, 2 * blk)

        def two_pairs(jj, carry):
            single_pass(4 * jj, 2, shift, 0, 0)
            single_pass(4 * jj + 2, 2, shift, 0, 0)
            return carry

        lax.fori_loop(0, i // 2, two_pairs, 0)

        @pl.when(i % 2 == 1)
        def _():
            single_pass(2 * (i - 1), 2, shift, 0, 0)

    lv = lam_ref[...]
    lam = (jnp.exp(jnp.sum(lv[0:1] * lv[1:2], axis=1, keepdims=True))
           - jnp.exp(jnp.sum(lv[2:3] * lv[3:4], axis=1, keepdims=True)) + lam_init)

    def finalize():
        mean_sq = []
        for hh in range(2):
            c0 = slice((2 * hh) * blk, (2 * hh + 1) * blk)
            c1 = slice((2 * hh + 1) * blk, (2 * hh + 2) * blk)
            inv0 = 1.0 / acc_ref[LANES:LANES + 1, c0]
            inv1 = lam / acc_ref[LANES:LANES + 1, c1]
            o = acc_ref[:LANES, c0] * inv0 - acc_ref[:LANES, c1] * inv1
            mean_sq.append(jnp.mean(o * o, axis=0, keepdims=True))
            y = o * lax.rsqrt(mean_sq[-1] + SUBLN_EPS) * gain_ref[...] * (1.0 - lam_init)
            o_ref[0, hh * blk:(hh + 1) * blk, :] = y.T.astype(BF16)
        return mean_sq

    attend(exact=False)
    mean_sq = finalize()
    finite = jnp.sum((mean_sq[0] + mean_sq[1]) * 0.0) == 0.0
    stale = jnp.logical_not((jnp.max(acc_ref[LANES:LANES + 1, :]) <= LAZY_LIMIT) & finite)

    @pl.when(stale)
    def _():
        attend(exact=True)
        finalize()


def _diff_call(qk3, vt4, lam_vecs, gain_col, lam_init):
    bsz, seq, _ = qk3.shape
    blk = TOKEN_TILE
    nblk = seq // blk
    k_lane_block = DIFF_WIDTH // LANES
    return pl.pallas_call(
        functools.partial(_diff_kernel, lam_init=lam_init, blk=blk),
        out_shape=jax.ShapeDtypeStruct((bsz, seq, DIFF_WIDTH), BF16),
        grid=(bsz, N_DIFF_HEADS, nblk // 2),
        in_specs=[pl.BlockSpec((4, HEAD_DIM), lambda b, h, i: (0, 0)),
                  pl.BlockSpec((LANES, 1), lambda b, h, i: (0, 0)),
                  pl.BlockSpec((1, 2 * blk, LANES), lambda b, h, i: (b, i, h)),
                  pl.BlockSpec((1, seq, LANES), lambda b, h, i: (b, 0, k_lane_block + h)),
                  pl.BlockSpec((1, nblk, LANES, blk), lambda b, h, i: (b, 0, h, 0))],
        out_specs=pl.BlockSpec((1, 2 * blk, LANES), lambda b, h, i: (b, i, h)),
        scratch_shapes=[pltpu.VMEM((4 * blk, LANES), BF16), pltpu.VMEM((ACC_ROWS, 4 * blk), F32)],
        compiler_params=_params(("parallel", "parallel", "arbitrary")),
        name="diff_attn",
    )(lam_vecs, gain_col, qk3, qk3, vt4)


def _dil_kernel(q_ref, k_ref, v_ref, o_ref, m_ref, l_ref, acc_ref):
    c = pl.program_id(2)
    base = c * DIL_CHUNK
    blk = DIL_BLOCK
    lane = lax.broadcasted_iota(jnp.int32, (blk, LANES), 1)
    v_first = lane < 64
    lane2 = lax.broadcasted_iota(jnp.int32, (2 * blk, LANES), 1)
    row2 = lax.broadcasted_iota(jnp.int32, (2 * blk, LANES), 0)
    q_keep = ((lane2 % 64) < 32) == (row2 < blk)
    qrow = lax.broadcasted_iota(jnp.int32, (2 * blk, 2 * blk), 0) % blk
    kcol = lax.broadcasted_iota(jnp.int32, (2 * blk, 2 * blk), 1)

    def band(qpos, kpos):
        dist = qpos + blk - kpos
        return (dist >= 0) & (dist <= blk)

    def score_cap(visible):
        return jnp.where(visible, jnp.inf, NEG_INF).astype(F32)

    pos4 = lambda idx: 32 * (idx // 32) + 4 * (idx % SUBLANES) + (idx % 32) // SUBLANES
    band1 = band(_true_pos(qrow), _true_pos(kcol))
    band4 = band(pos4(qrow), blk * (kcol // blk) + pos4(kcol % blk))
    band16 = band(qrow, kcol)

    def strip(start):
        return pl.ds(start if isinstance(start, int) else pl.multiple_of(start, SUBLANES), SUBLANES)

    def strips(ref, starts):
        return jnp.concatenate([ref[0, strip(s), :] for s in starts], axis=0)

    ones_a = jnp.where(v_first, 1.0, 0.0).astype(BF16)
    ones_b = jnp.where(v_first, 0.0, 1.0).astype(BF16)

    def value_block(v):
        zero = jnp.zeros_like(v)
        return (jnp.concatenate([jnp.where(v_first, v, zero), ones_a], axis=1),
                jnp.concatenate([jnp.where(v_first, zero, v), ones_b], axis=1))

    def attend(q, k, v, cap):
        qb = q.astype(BF16)
        q2 = jnp.concatenate([qb, qb], axis=0)
        q2 = jnp.where(q_keep, q2, jnp.zeros_like(q2))
        s = lax.dot_general(q2, k, NT_DIMS, preferred_element_type=F32)
        s = jnp.minimum(s, cap)
        m = jnp.max(s, axis=1, keepdims=True)
        p = jnp.exp2(s - m).astype(BF16)
        (va_prev, vb_prev), (va_own, vb_own) = v
        p_cat = jnp.concatenate([p[:blk], p[blk:]], axis=1)
        v_cat = jnp.concatenate([va_prev, va_own, vb_prev, vb_own], axis=0)
        num = jnp.dot(p_cat, v_cat, preferred_element_type=F32)
        m_pair = jnp.where(v_first, jnp.broadcast_to(m[:blk], (blk, LANES)), jnp.broadcast_to(m[blk:], (blk, LANES)))
        return m_pair, num[:, LANES:], num[:, :LANES]

    def merge(q_starts, m, den, num):
        for t, s0 in enumerate(q_starts):
            rows = strip(s0)
            sl = slice(t * SUBLANES, (t + 1) * SUBLANES)
            m_st = m_ref[rows, :]
            m_new = jnp.maximum(m_st, m[sl])
            w_st = jnp.exp2(m_st - m_new)
            w_new = jnp.exp2(m[sl] - m_new)
            m_ref[rows, :] = m_new
            l_ref[rows, :] = w_st * l_ref[rows, :] + w_new * den[sl]
            acc_ref[rows, :] = w_st * acc_ref[rows, :] + w_new * num[sl]

    first_key = jnp.where(c > 0, 0, blk)

    cap_first = lambda in_band: score_cap(in_band & (kcol >= first_key))
    both = lambda prev, own: jnp.concatenate([prev, own], axis=0)
    gather = lambda ref, starts: strips(ref, starts).astype(BF16)

    cap_a, cap_b = cap_first(band1), score_cap(band1)
    start0 = jnp.maximum(base - blk, 0)
    load = lambda ref, start: ref[0, pl.ds(pl.multiple_of(start, blk), blk), :].astype(BF16)
    k_prev, v_prev = load(k_ref, start0), value_block(load(v_ref, start0))
    for u in range(DIL_MAX):
        rows = slice(u * blk, (u + 1) * blk)
        k_own, v_own = load(k_ref, base + u * blk), value_block(load(v_ref, base + u * blk))
        m, den, num = attend(q_ref[0, rows, :], both(k_prev, k_own), (v_prev, v_own), cap_a if u == 0 else cap_b)
        m_ref[rows, :] = m
        l_ref[rows, :] = den
        acc_ref[rows, :] = num
        k_prev, v_prev = k_own, v_own

    cap_a, cap_b = cap_first(band4), score_cap(band4)
    start0 = jnp.maximum(base - 512, 0)
    for r4 in range(4):
        block_rows = lambda bq: [blk * (4 * bq + ul) + SUBLANES * (4 * a + r4) for ul in range(4) for a in range(4)]
        prev = [start0 + s for s in block_rows(0)]
        k_prev, v_prev = gather(k_ref, prev), value_block(gather(v_ref, prev))
        for bq in range(4):
            rel = block_rows(bq)
            own = [base + s for s in rel]
            k_own, v_own = gather(k_ref, own), value_block(gather(v_ref, own))
            m, den, num = attend(strips(q_ref, rel), both(k_prev, k_own), (v_prev, v_own),
                                 cap_a if bq == 0 else cap_b)
            merge(rel, m, den, num)
            k_prev, v_prev = k_own, v_own

    cap_a = cap_first(band16)
    start0 = jnp.maximum(base - DIL_CHUNK, 0)
    for r in range(DIL_MAX):
        rel = [blk * u + SUBLANES * r for u in range(DIL_MAX)]
        k = both(gather(k_ref, [start0 + s for s in rel]), gather(k_ref, [base + s for s in rel]))
        v = (value_block(gather(v_ref, [start0 + s for s in rel])), value_block(gather(v_ref, [base + s for s in rel])))
        m, den, num = attend(strips(q_ref, rel), k, v, cap_a)
        merge(rel, m, den, num)

    o_ref[0] = acc_ref[...] / l_ref[...]


def _dil_call(dil3):
    bsz, seq, _ = dil3.shape
    pairs = DIL_WIDTH // LANES
    return pl.pallas_call(
        _dil_kernel,
        out_shape=jax.ShapeDtypeStruct((bsz, seq, DIL_WIDTH), F32),
        grid=(bsz, pairs, seq // DIL_CHUNK),
        in_specs=[pl.BlockSpec((1, DIL_CHUNK, LANES), lambda b, g, c: (b, c, g)),
                  pl.BlockSpec((1, seq, LANES), lambda b, g, c: (b, 0, pairs + g)),
                  pl.BlockSpec((1, seq, LANES), lambda b, g, c: (b, 0, 2 * pairs + g))],
        out_specs=pl.BlockSpec((1, DIL_CHUNK, LANES), lambda b, g, c: (b, c, g)),
        scratch_shapes=[pltpu.VMEM((DIL_CHUNK, LANES), F32)] * 3,
        compiler_params=_params(("parallel", "parallel", "arbitrary")),
        name="dilated",
    )(dil3, dil3, dil3)


def _rope_column_order():
    half = HEAD_DIM // 2
    cols = []
    for start in (0, 512, 1536, 2048):
        for g in range(512 // LANES):
            a = start + g * LANES
            b = a + HEAD_DIM
            cols += list(range(a, a + half)) + list(range(b, b + half))
            cols += list(range(a + half, a + HEAD_DIM)) + list(range(b + half, b + HEAD_DIM))
    cols += list(range(2560, 3072))
    return np.asarray(cols, dtype=np.int32)


def kernel(x, positions, ffn1_norm, ffn1_gate, ffn1_up, ffn1_down, mix_norm, w_in, lambda_q1, lambda_k1, lambda_q2, lambda_k2, subln_gain, dil_gain, w_out, ffn2_norm, ffn2_gate, ffn2_up, ffn2_down, final_norm):
    bsz, seq, _ = x.shape
    depth = w_in.shape[0]
    n = bsz * seq
    assert seq % DIL_CHUNK == 0 and seq % (2 * TOKEN_TILE) == 0

    bf = lambda w: w.astype(BF16)
    w_main = bf(w_in[:, :, _rope_column_order()])
    w_vt = bf(jnp.swapaxes(w_in[:, :, 1024:1536], 1, 2))
    g1, u1, d1 = bf(ffn1_gate), bf(ffn1_up), bf(ffn1_down)
    g2, u2, d2 = bf(ffn2_gate), bf(ffn2_up), bf(ffn2_down)
    wo = bf(w_out)
    n1 = ffn1_norm[:, None, :]
    nm = mix_norm[:, None, :]
    n2 = ffn2_norm[:, None, :]
    dg = dil_gain[:, None, :]

    cos, sin = _rope_tables(_to_residue_major(positions, 0))
    h = x.reshape(n // DIL_BLOCK, SUBLANES, DIL_MAX, D_MODEL)
    for l in range(depth):
        lam_init = 0.8 - 0.6 * math.exp(-0.3 * l)
        h = _ffn_call(h, l, n1, g1, u1, d1, tokens_in=l == 0)
        qk, dil, vt = _proj_call(h, l, nm, w_main, w_vt, cos, sin, seq)
        vt4 = vt.reshape(bsz, seq // TOKEN_TILE, DIFF_WIDTH, TOKEN_TILE)
        lam_vecs = jnp.stack([lambda_q1[l], lambda_k1[l], lambda_q2[l], lambda_k2[l]])
        d_out = _diff_call(qk.reshape(bsz, seq, QK_WIDTH), vt4, lam_vecs, subln_gain[l][:, None], lam_init)
        a_mix = _dil_call(dil.reshape(bsz, seq, DIL_QKV_WIDTH))
        final = final_norm[None, :] if l == depth - 1 else None
        h = _ffn_call(h, l, n2, g2, u2, d2,
                      mix=(d_out.reshape(n, DIFF_WIDTH), a_mix.reshape(n, DIL_WIDTH), dg, wo),
                      final=final, tokens_out=l == depth - 1)
    return h.reshape(bsz, seq, D_MODEL)
```

```python
import functools
import math

import numpy as np
import jax
import jax.numpy as jnp
from jax import lax
from jax.experimental import pallas as pl
from jax.experimental.pallas import tpu as pltpu

D_MODEL = 1024
HEAD_DIM = 64
N_DIFF_HEADS = 4
DIFF_WIDTH = N_DIFF_HEADS * 2 * HEAD_DIM
DIL_WIDTH = 8 * HEAD_DIM
DIL_BLOCK = 128
DIL_MAX = 16
D_FF = 2816
ROPE_THETA = 10000.0
EPS = 1e-6
SUBLN_EPS = 1e-5
NEG_INF = -1e30
Q_SCALE = HEAD_DIM ** -0.5 * math.log2(math.e)
LAZY_LIMIT = 2.0 ** 90

LANES = 128
SUBLANES = 8
MXU_WIDTH = 256
VMEM_LIMIT = 56 * 1024 * 1024

QK_WIDTH = 2 * DIFF_WIDTH
DIL_QKV_WIDTH = 3 * DIL_WIDTH
PROJ_WIDTH = QK_WIDTH + DIL_QKV_WIDTH
SECTION = 512
ROPE_WIDTH = 4 * SECTION
ROPE_HALF = HEAD_DIM // 2

FF_CHUNK = 256
TOKEN_TILE = 512
PROJ_TILE = 1024
FFN_TILE = 1024
ACC_ROWS = LANES + SUBLANES
DIL_CHUNK = DIL_BLOCK * DIL_MAX

BF16 = jnp.bfloat16
F32 = jnp.float32
NT_DIMS = (((1,), (1,)), ((), ()))


def _rmsnorm(xf, g, eps):
    return xf * lax.rsqrt(jnp.mean(xf * xf, axis=-1, keepdims=True) + eps) * g


def _params(semantics):
    return pltpu.CompilerParams(dimension_semantics=semantics, vmem_limit_bytes=VMEM_LIMIT)


def _resident(block_shape, index_map):
    return pl.BlockSpec(block_shape, index_map, pipeline_mode=pl.Buffered(1))


def _true_pos(idx):
    return (idx // DIL_BLOCK) * DIL_BLOCK + DIL_MAX * (idx % SUBLANES) + (idx % DIL_BLOCK) // SUBLANES


def _to_residue_major(a, axis):
    shape = a.shape
    a = a.reshape(shape[:axis] + (shape[axis] // DIL_BLOCK, SUBLANES, DIL_MAX) + shape[axis + 1:])
    return jnp.swapaxes(a, axis + 1, axis + 2).reshape(shape)


def _rope_kernel(pos_ref, inv_ref, sign_ref, cos_ref, sin_ref):
    ang = pos_ref[...].astype(F32) * inv_ref[...]
    cos_ref[...] = jnp.cos(ang)
    sin_ref[...] = jnp.sin(ang) * sign_ref[...]


def _rope_tables(positions):
    seq = positions.shape[0]
    inv = 1.0 / (ROPE_THETA ** (jnp.arange(0, HEAD_DIM, 2, dtype=F32) / HEAD_DIM))
    inv_t = jnp.tile(inv, 4)[None, :]
    sign = jnp.concatenate([-jnp.ones((HEAD_DIM,), F32), jnp.ones((HEAD_DIM,), F32)])[None, :]
    rows = min(seq, 1024)
    return pl.pallas_call(
        _rope_kernel,
        out_shape=(jax.ShapeDtypeStruct((seq, LANES), F32),) * 2,
        grid=(seq // rows,),
        in_specs=[pl.BlockSpec((rows, 1), lambda i: (i, 0)),
                  pl.BlockSpec((1, LANES), lambda i: (0, 0)),
                  pl.BlockSpec((1, LANES), lambda i: (0, 0))],
        out_specs=(pl.BlockSpec((rows, LANES), lambda i: (i, 0)),) * 2,
        compiler_params=_params(("arbitrary",)),
        name="rope_tables",
    )(positions.reshape(seq, 1), inv_t, sign)


def _swiglu_residual(xf, g, wg_ref, wu_ref, wd_ref, h_ref):
    xn = _rmsnorm(xf, g, EPS).astype(BF16)
    for c in range(D_FF // FF_CHUNK):
        sl = slice(c * FF_CHUNK, (c + 1) * FF_CHUNK)
        a = jnp.dot(xn, wg_ref[:, sl], preferred_element_type=F32)
        u = jnp.dot(xn, wu_ref[:, sl], preferred_element_type=F32)
        h_ref[:, sl] = (a * jax.nn.sigmoid(a) * u).astype(BF16)
    y = jnp.dot(h_ref[...], wd_ref[...], preferred_element_type=F32)
    return xf + 0.5 * y


def _ffn_kernel(*refs, has_mix, has_final, tokens_in, tokens_out):
    refs = list(refs)
    x_ref = refs.pop(0)
    if has_mix:
        d_ref, a_ref, dg_ref, wo_ref = (refs.pop(0) for _ in range(4))
    g_ref, wg_ref, wu_ref, wd_ref = (refs.pop(0) for _ in range(4))
    if has_final:
        fin_ref = refs.pop(0)
    o_ref, h_ref = refs
    xf = x_ref[...]
    if tokens_in:
        xf = jnp.swapaxes(xf, 1, 2).reshape(-1, D_MODEL)
    if has_mix:
        a_out = _rmsnorm(a_ref[...], dg_ref[...], EPS).astype(BF16)
        xf = xf + jnp.dot(d_ref[...], wo_ref[:DIFF_WIDTH, :], preferred_element_type=F32)
        xf = xf + jnp.dot(a_out, wo_ref[DIFF_WIDTH:, :], preferred_element_type=F32)
    out = _swiglu_residual(xf, g_ref[...], wg_ref, wu_ref, wd_ref, h_ref)
    if has_final:
        out = _rmsnorm(out, fin_ref[...], EPS)
    if tokens_out:
        out = jnp.swapaxes(out.reshape(-1, DIL_MAX, SUBLANES, D_MODEL), 1, 2)
    o_ref[...] = out


def _ffn_call(x2, layer, norm, wg, wu, wd, mix=None, final=None, tokens_in=False, tokens_out=False):
    n = x2.size // D_MODEL
    tm = FFN_TILE
    row = lambda i: (i, 0)
    const2 = lambda i: (0, 0)
    lay3 = lambda i: (layer, 0, 0)
    tok_block = (tm // DIL_BLOCK, SUBLANES, DIL_MAX, D_MODEL)
    tok_spec = pl.BlockSpec(tok_block, lambda i: (i, 0, 0, 0))
    args = [x2]
    specs = [tok_spec if tokens_in else pl.BlockSpec((tm, D_MODEL), row)]
    if mix is not None:
        d_out, a_mix, dil_gain, w_out = mix
        args += [d_out, a_mix, dil_gain, w_out]
        specs += [pl.BlockSpec((tm, DIFF_WIDTH), row), pl.BlockSpec((tm, DIL_WIDTH), row),
                  pl.BlockSpec((None, 1, DIL_WIDTH), lay3),
                  _resident((None, D_MODEL, D_MODEL), lay3)]
    args += [norm, wg, wu, wd]
    specs += [pl.BlockSpec((None, 1, D_MODEL), lay3),
              _resident((None, D_MODEL, D_FF), lay3),
              _resident((None, D_MODEL, D_FF), lay3),
              _resident((None, D_FF, D_MODEL), lay3)]
    if final is not None:
        args.append(final)
        specs.append(pl.BlockSpec((1, D_MODEL), const2))
    return pl.pallas_call(
        functools.partial(_ffn_kernel, has_mix=mix is not None, has_final=final is not None,
                          tokens_in=tokens_in, tokens_out=tokens_out),
        out_shape=jax.ShapeDtypeStruct((n // DIL_BLOCK, SUBLANES, DIL_MAX, D_MODEL) if tokens_out else (n, D_MODEL), F32),
        grid=(n // tm,),
        in_specs=specs,
        out_specs=tok_spec if tokens_out else pl.BlockSpec((tm, D_MODEL), row),
        scratch_shapes=[pltpu.VMEM((tm, D_FF), BF16)],
        compiler_params=_params(("parallel",)),
        name="mix_ffn" if mix is not None else "ffn",
    )(*args)


def _proj_kernel(x_ref, g_ref, w_ref, wvt_ref, cos_ref, sin_ref, qk_ref, dil_ref, vt_ref):
    xn = _rmsnorm(x_ref[...], g_ref[...], EPS).astype(BF16)
    cos = cos_ref[...]
    sin = sin_ref[...]
    for c in range(PROJ_WIDTH // MXU_WIDTH):
        lo = c * MXU_WIDTH
        t = jnp.dot(xn, w_ref[:, lo:lo + MXU_WIDTH], preferred_element_type=F32)
        if lo < ROPE_WIDTH:
            scale = Q_SCALE if (lo // SECTION) % 2 == 0 else 1.0
            halves = []
            for half in range(MXU_WIDTH // LANES):
                th = t[:, half * LANES:(half + 1) * LANES]
                halves.append((th * cos + pltpu.roll(th, HEAD_DIM, 1) * sin) * scale)
            t = jnp.concatenate(halves, axis=1)
        if lo < QK_WIDTH:
            qk_ref[:, lo:lo + MXU_WIDTH] = t.astype(BF16)
        else:
            dil_ref[:, lo - QK_WIDTH:lo - QK_WIDTH + MXU_WIDTH] = t
    vt = lax.dot_general(wvt_ref[...], xn, NT_DIMS, preferred_element_type=F32)
    for t in range(vt_ref.shape[0]):
        vt_ref[t] = vt[:, t * TOKEN_TILE:(t + 1) * TOKEN_TILE].astype(BF16)


def _proj_call(x2, layer, norm, w_main, w_vt, cos, sin, seq):
    n = x2.shape[0]
    tm = PROJ_TILE
    tiles_per_seq = seq // tm
    row = lambda i: (i, 0)
    lay3 = lambda i: (layer, 0, 0)
    pos = lambda i: (i % tiles_per_seq, 0)
    return pl.pallas_call(
        _proj_kernel,
        out_shape=(jax.ShapeDtypeStruct((n, QK_WIDTH), BF16),
                   jax.ShapeDtypeStruct((n, DIL_QKV_WIDTH), F32),
                   jax.ShapeDtypeStruct((n // TOKEN_TILE, DIFF_WIDTH, TOKEN_TILE), BF16)),
        grid=(n // tm,),
        in_specs=[pl.BlockSpec((tm, D_MODEL), row),
                  pl.BlockSpec((None, 1, D_MODEL), lay3),
                  _resident((None, D_MODEL, PROJ_WIDTH), lay3),
                  _resident((None, DIFF_WIDTH, D_MODEL), lay3),
                  pl.BlockSpec((tm, LANES), pos),
                  pl.BlockSpec((tm, LANES), pos)],
        out_specs=(pl.BlockSpec((tm, QK_WIDTH), row),
                   pl.BlockSpec((tm, DIL_QKV_WIDTH), row),
                   pl.BlockSpec((tm // TOKEN_TILE, DIFF_WIDTH, TOKEN_TILE), lambda i: (i, 0, 0))),
        compiler_params=_params(("parallel",)),
        name="proj_rope",
    )(x2, norm, w_main, w_vt, cos, sin)


def _diff_kernel(lam_ref, gain_ref, q_ref, k_ref, vt_ref, o_ref, qall_ref, acc_ref, *, lam_init, blk):
    i = pl.program_id(2)
    ncol = 4 * blk
    lane = lax.broadcasted_iota(jnp.int32, (blk, LANES), 1)
    is_first = (lane % HEAD_DIM) < ROPE_HALF
    for hh in range(2):
        q = q_ref[0, hh * blk:(hh + 1) * blk, :]
        zero = jnp.zeros_like(q)
        qall_ref[(2 * hh) * blk:(2 * hh + 1) * blk, :] = jnp.where(is_first, q, zero)
        qall_ref[(2 * hh + 1) * blk:(2 * hh + 2) * blk, :] = jnp.where(is_first, zero, q)

    def scores(j, nkb, col0, masked_cols):
        keys = nkb * blk
        kb = k_ref[0, pl.ds(pl.multiple_of(j * blk, blk), keys), :]
        s = lax.dot_general(kb, qall_ref[col0:, :], NT_DIMS, preferred_element_type=F32)
        if masked_cols:
            key_idx = lax.broadcasted_iota(jnp.int32, (keys, masked_cols), 0)
            col_idx = lax.broadcasted_iota(jnp.int32, (keys, masked_cols), 1)
            causal = _true_pos(key_idx) <= _true_pos(col_idx % blk)
            masked = jnp.where(causal, s[:, :masked_cols], NEG_INF)
            s = masked if masked_cols == s.shape[1] else jnp.concatenate([masked, s[:, masked_cols:]], axis=1)
        return s

    def values(j, nkb, p):
        pb = p.astype(BF16)
        num = sum(jnp.dot(vt_ref[0, j + t], pb[t * blk:(t + 1) * blk], preferred_element_type=F32)
                  for t in range(nkb))
        return num, jnp.sum(p, axis=0, keepdims=True)

    def exact_block(j, nkb, m_old, col0, masked_cols):
        s = scores(j, nkb, col0, masked_cols)
        m_prev = m_old[:, col0:]
        m_new = jnp.maximum(m_prev, jnp.max(s, axis=0, keepdims=True))
        alpha = jnp.exp2(m_prev - m_new)
        num, den = values(j, nkb, jnp.exp2(s - m_new))
        acc_ref[:LANES, col0:] = alpha * acc_ref[:LANES, col0:] + num
        acc_ref[LANES:LANES + 1, col0:] = alpha * acc_ref[LANES:LANES + 1, col0:] + den
        if col0:
            m_new = jnp.concatenate([m_old[:, :col0], m_new], axis=1)
        return m_new

    def single_pass(j, nkb, shift, col0, masked_cols, first=False):
        s = scores(j, nkb, col0, masked_cols)
        if first:
            shift = s[0:1, :]
        num, den = values(j, nkb, jnp.exp2(s - shift[:, col0:]))
        if first:
            acc_ref[:LANES, :] = num
            acc_ref[LANES:LANES + 1, :] = den
        else:
            acc_ref[:LANES, col0:] += num
            acc_ref[LANES:LANES + 1, col0:] += den
        return shift

    def attend(exact):
        if exact:
            acc_ref[...] = jnp.zeros_like(acc_ref)
            m = jnp.full((1, ncol), NEG_INF, F32)
            m = exact_block(2 * i, 1, m, 0, 2 * blk)
            m = exact_block(2 * i + 1, 1, m, 2 * blk, 2 * blk)
            lax.fori_loop(0, i, lambda jj, mm: exact_block(2 * jj, 2, mm, 0, 0), m)
            return
        shift = single_pass(2 * i, 1, None, 0, 2 * blk, first=True)
        single_pass(2 * i + 1, 1, shift, 2 * blk, 2 * blk)

        def two_pairs(jj, carry):
            single_pass(4 * jj, 2, shift, 0, 0)
            single_pass(4 * jj + 2, 2, shift, 0, 0)
            return carry

        lax.fori_loop(0, i // 2, two_pairs, 0)

        @pl.when(i % 2 == 1)
        def _():
            single_pass(2 * (i - 1), 2, shift, 0, 0)

    lv = lam_ref[...]
    lam = (jnp.exp(jnp.sum(lv[0:1] * lv[1:2], axis=1, keepdims=True))
           - jnp.exp(jnp.sum(lv[2:3] * lv[3:4], axis=1, keepdims=True)) + lam_init)

    def finalize():
        mean_sq = []
        for hh in range(2):
            c0 = slice((2 * hh) * blk, (2 * hh + 1) * blk)
            c1 = slice((2 * hh + 1) * blk, (2 * hh + 2) * blk)
            inv0 = 1.0 / acc_ref[LANES:LANES + 1, c0]
            inv1 = lam / acc_ref[LANES:LANES + 1, c1]
            o = acc_ref[:LANES, c0] * inv0 - acc_ref[:LANES, c1] * inv1
            mean_sq.append(jnp.mean(o * o, axis=0, keepdims=True))
            y = o * lax.rsqrt(mean_sq[-1] + SUBLN_EPS) * gain_ref[...] * (1.0 - lam_init)
            o_ref[0, hh * blk:(hh + 1) * blk, :] = y.T.astype(BF16)
        return mean_sq

    attend(exact=False)
    mean_sq = finalize()
    finite = jnp.sum((mean_sq[0] + mean_sq[1]) * 0.0) == 0.0
    stale = jnp.logical_not((jnp.max(acc_ref[LANES:LANES + 1, :]) <= LAZY_LIMIT) & finite)

    @pl.when(stale)
    def _():
        attend(exact=True)
        finalize()


def _diff_call(qk3, vt4, lam_vecs, gain_col, lam_init):
    bsz, seq, _ = qk3.shape
    blk = TOKEN_TILE
    nblk = seq // blk
    k_lane_block = DIFF_WIDTH // LANES
    return pl.pallas_call(
        functools.partial(_diff_kernel, lam_init=lam_init, blk=blk),
        out_shape=jax.ShapeDtypeStruct((bsz, seq, DIFF_WIDTH), BF16),
        grid=(bsz, N_DIFF_HEADS, nblk // 2),
        in_specs=[pl.BlockSpec((4, HEAD_DIM), lambda b, h, i: (0, 0)),
                  pl.BlockSpec((LANES, 1), lambda b, h, i: (0, 0)),
                  pl.BlockSpec((1, 2 * blk, LANES), lambda b, h, i: (b, i, h)),
                  pl.BlockSpec((1, seq, LANES), lambda b, h, i: (b, 0, k_lane_block + h)),
                  pl.BlockSpec((1, nblk, LANES, blk), lambda b, h, i: (b, 0, h, 0))],
        out_specs=pl.BlockSpec((1, 2 * blk, LANES), lambda b, h, i: (b, i, h)),
        scratch_shapes=[pltpu.VMEM((4 * blk, LANES), BF16), pltpu.VMEM((ACC_ROWS, 4 * blk), F32)],
        compiler_params=_params(("parallel", "parallel", "arbitrary")),
        name="diff_attn",
    )(lam_vecs, gain_col, qk3, qk3, vt4)


def _dil_kernel(q_ref, k_ref, v_ref, o_ref, m_ref, l_ref, acc_ref):
    c = pl.program_id(2)
    base = c * DIL_CHUNK
    blk = DIL_BLOCK
    lane = lax.broadcasted_iota(jnp.int32, (blk, LANES), 1)
    v_first = lane < HEAD_DIM
    lane2 = lax.broadcasted_iota(jnp.int32, (2 * blk, LANES), 1)
    row2 = lax.broadcasted_iota(jnp.int32, (2 * blk, LANES), 0)
    q_keep = ((lane2 % HEAD_DIM) < ROPE_HALF) == (row2 < blk)
    qrow = lax.broadcasted_iota(jnp.int32, (2 * blk, 2 * blk), 0) % blk
    kcol = lax.broadcasted_iota(jnp.int32, (2 * blk, 2 * blk), 1)

    def band(qpos, kpos):
        dist = qpos + blk - kpos
        return (dist >= 0) & (dist <= blk)

    def score_cap(visible):
        return jnp.where(visible, jnp.inf, NEG_INF).astype(F32)

    pos4 = lambda idx: 32 * (idx // 32) + 4 * (idx % SUBLANES) + (idx % 32) // SUBLANES
    band1 = band(_true_pos(qrow), _true_pos(kcol))
    band4 = band(pos4(qrow), blk * (kcol // blk) + pos4(kcol % blk))
    band16 = band(qrow, kcol)

    def strip(start):
        return pl.ds(start if isinstance(start, int) else pl.multiple_of(start, SUBLANES), SUBLANES)

    def strips(ref, starts):
        return jnp.concatenate([ref[0, strip(s), :] for s in starts], axis=0)

    ones_a = jnp.where(v_first, 1.0, 0.0).astype(BF16)
    ones_b = jnp.where(v_first, 0.0, 1.0).astype(BF16)

    def value_block(v):
        zero = jnp.zeros_like(v)
        return (jnp.concatenate([jnp.where(v_first, v, zero), ones_a], axis=1),
                jnp.concatenate([jnp.where(v_first, zero, v), ones_b], axis=1))

    def attend(q, k, v, cap):
        qb = q.astype(BF16)
        q2 = jnp.concatenate([qb, qb], axis=0)
        q2 = jnp.where(q_keep, q2, jnp.zeros_like(q2))
        s = lax.dot_general(q2, k, NT_DIMS, preferred_element_type=F32)
        s = jnp.minimum(s, cap)
        m = jnp.max(s, axis=1, keepdims=True)
        p = jnp.exp2(s - m).astype(BF16)
        (va_prev, vb_prev), (va_own, vb_own) = v
        p_cat = jnp.concatenate([p[:blk], p[blk:]], axis=1)
        v_cat = jnp.concatenate([va_prev, va_own, vb_prev, vb_own], axis=0)
        num = jnp.dot(p_cat, v_cat, preferred_element_type=F32)
        m_pair = jnp.where(v_first, jnp.broadcast_to(m[:blk], (blk, LANES)), jnp.broadcast_to(m[blk:], (blk, LANES)))
        return m_pair, num[:, LANES:], num[:, :LANES]

    def merge(q_starts, m, den, num):
        for t, s0 in enumerate(q_starts):
            rows = strip(s0)
            sl = slice(t * SUBLANES, (t + 1) * SUBLANES)
            m_st = m_ref[rows, :]
            m_new = jnp.maximum(m_st, m[sl])
            w_st = jnp.exp2(m_st - m_new)
            w_new = jnp.exp2(m[sl] - m_new)
            m_ref[rows, :] = m_new
            l_ref[rows, :] = w_st * l_ref[rows, :] + w_new * den[sl]
            acc_ref[rows, :] = w_st * acc_ref[rows, :] + w_new * num[sl]

    first_key = jnp.where(c > 0, 0, blk)

    cap_first = lambda in_band: score_cap(in_band & (kcol >= first_key))
    both = lambda prev, own: jnp.concatenate([prev, own], axis=0)
    gather = lambda ref, starts: strips(ref, starts).astype(BF16)

    cap_a, cap_b = cap_first(band1), score_cap(band1)
    start0 = jnp.maximum(base - blk, 0)
    load = lambda ref, start: ref[0, pl.ds(pl.multiple_of(start, blk), blk), :].astype(BF16)
    k_prev, v_prev = load(k_ref, start0), value_block(load(v_ref, start0))
    for u in range(DIL_MAX):
        rows = slice(u * blk, (u + 1) * blk)
        k_own, v_own = load(k_ref, base + u * blk), value_block(load(v_ref, base + u * blk))
        m, den, num = attend(q_ref[0, rows, :], both(k_prev, k_own), (v_prev, v_own), cap_a if u == 0 else cap_b)
        m_ref[rows, :] = m
        l_ref[rows, :] = den
        acc_ref[rows, :] = num
        k_prev, v_prev = k_own, v_own

    cap_a, cap_b = cap_first(band4), score_cap(band4)
    start0 = jnp.maximum(base - 512, 0)
    for r4 in range(4):
        block_rows = lambda bq: [blk * (4 * bq + ul) + SUBLANES * (4 * a + r4) for ul in range(4) for a in range(4)]
        prev = [start0 + s for s in block_rows(0)]
        k_prev, v_prev = gather(k_ref, prev), value_block(gather(v_ref, prev))
        for bq in range(4):
            rel = block_rows(bq)
            own = [base + s for s in rel]
            k_own, v_own = gather(k_ref, own), value_block(gather(v_ref, own))
            m, den, num = attend(strips(q_ref, rel), both(k_prev, k_own), (v_prev, v_own),
                                 cap_a if bq == 0 else cap_b)
            merge(rel, m, den, num)
            k_prev, v_prev = k_own, v_own

    cap_a = cap_first(band16)
    start0 = jnp.maximum(base - DIL_CHUNK, 0)
    for r in range(DIL_MAX):
        rel = [blk * u + SUBLANES * r for u in range(DIL_MAX)]
        k = both(gather(k_ref, [start0 + s for s in rel]), gather(k_ref, [base + s for s in rel]))
        v = (value_block(gather(v_ref, [start0 + s for s in rel])), value_block(gather(v_ref, [base + s for s in rel])))
        m, den, num = attend(strips(q_ref, rel), k, v, cap_a)
        merge(rel, m, den, num)

    o_ref[0] = acc_ref[...] / l_ref[...]


def _dil_call(dil3):
    bsz, seq, _ = dil3.shape
    pairs = DIL_WIDTH // LANES
    return pl.pallas_call(
        _dil_kernel,
        out_shape=jax.ShapeDtypeStruct((bsz, seq, DIL_WIDTH), F32),
        grid=(bsz, pairs, seq // DIL_CHUNK),
        in_specs=[pl.BlockSpec((1, DIL_CHUNK, LANES), lambda b, g, c: (b, c, g)),
                  pl.BlockSpec((1, seq, LANES), lambda b, g, c: (b, 0, pairs + g)),
                  pl.BlockSpec((1, seq, LANES), lambda b, g, c: (b, 0, 2 * pairs + g))],
        out_specs=pl.BlockSpec((1, DIL_CHUNK, LANES), lambda b, g, c: (b, c, g)),
        scratch_shapes=[pltpu.VMEM((DIL_CHUNK, LANES), F32)] * 3,
        compiler_params=_params(("parallel", "parallel", "arbitrary")),
        name="dilated",
    )(dil3, dil3, dil3)


def _rope_column_order():
    half = HEAD_DIM // 2
    cols = []
    for start in (0, 512, 1536, 2048):
        for g in range(512 // LANES):
            a = start + g * LANES
            b = a + HEAD_DIM
            cols += list(range(a, a + half)) + list(range(b, b + half))
            cols += list(range(a + half, a + HEAD_DIM)) + list(range(b + half, b + HEAD_DIM))
    cols += list(range(2560, 3072))
    return np.asarray(cols, dtype=np.int32)


def kernel(x, positions, ffn1_norm, ffn1_gate, ffn1_up, ffn1_down, mix_norm, w_in, lambda_q1, lambda_k1, lambda_q2, lambda_k2, subln_gain, dil_gain, w_out, ffn2_norm, ffn2_gate, ffn2_up, ffn2_down, final_norm):
    bsz, seq, _ = x.shape
    depth = w_in.shape[0]
    n = bsz * seq
    assert seq % DIL_CHUNK == 0 and seq % (2 * TOKEN_TILE) == 0

    bf = lambda w: w.astype(BF16)
    w_main = bf(w_in[:, :, _rope_column_order()])
    w_vt = bf(jnp.swapaxes(w_in[:, :, 1024:1536], 1, 2))
    g1, u1, d1 = bf(ffn1_gate), bf(ffn1_up), bf(ffn1_down)
    g2, u2, d2 = bf(ffn2_gate), bf(ffn2_up), bf(ffn2_down)
    wo = bf(w_out)
    n1 = ffn1_norm[:, None, :]
    nm = mix_norm[:, None, :]
    n2 = ffn2_norm[:, None, :]
    dg = dil_gain[:, None, :]

    cos, sin = _rope_tables(_to_residue_major(positions, 0))
    h = x.reshape(n // DIL_BLOCK, SUBLANES, DIL_MAX, D_MODEL)
    for l in range(depth):
        lam_init = 0.8 - 0.6 * math.exp(-0.3 * l)
        h = _ffn_call(h, l, n1, g1, u1, d1, tokens_in=l == 0)
        qk, dil, vt = _proj_call(h, l, nm, w_main, w_vt, cos, sin, seq)
        vt4 = vt.reshape(bsz, seq // TOKEN_TILE, DIFF_WIDTH, TOKEN_TILE)
        lam_vecs = jnp.stack([lambda_q1[l], lambda_k1[l], lambda_q2[l], lambda_k2[l]])
        d_out = _diff_call(qk.reshape(bsz, seq, QK_WIDTH), vt4, lam_vecs, subln_gain[l][:, None], lam_init)
        a_mix = _dil_call(dil.reshape(bsz, seq, DIL_QKV_WIDTH))
        final = final_norm[None, :] if l == depth - 1 else None
        h = _ffn_call(h, l, n2, g2, u2, d2,
                      mix=(d_out.reshape(n, DIFF_WIDTH), a_mix.reshape(n, DIL_WIDTH), dg, wo),
                      final=final, tokens_out=l == depth - 1)
    return h.reshape(bsz, seq, D_MODEL)
```
